```python
import math
import jax, jax.numpy as jnp
from jax import lax
import numpy as np

D_MODEL = 1024
BATCH = 8
SEQ = 4096
DEPTH = 2

N_EVEN = (DEPTH + 1) // 2
N_ODD = DEPTH // 2

DA_HEADS = 4
DA_DIM = 64
DA_VDIM = 2 * DA_DIM
RET_HEADS = 4
RET_QK = 64
RET_V = 128
RET_CHUNK = 128
RET_THETA = 10000.0
GLA_HEADS = 4
GLA_QK = 128
GLA_V = 256
GLA_RANK = 16
GLA_TAU = 16.0
GLA_CHUNK = 64
D_FF = 2816
ROPE_THETA = 500000.0
ROPE_FRAC = 4
Q_BLOCK = 128
EPS = 1e-6

AB_SPLITS = (DA_HEADS * 2 * DA_DIM, DA_HEADS * 2 * DA_DIM, DA_HEADS * DA_VDIM,
             RET_HEADS * RET_QK, RET_HEADS * RET_QK, RET_HEADS * RET_V, RET_HEADS * RET_V)
AB_IN = sum(AB_SPLITS)
AB_OUT = DA_HEADS * DA_VDIM + RET_HEADS * RET_V
C_SPLITS = (GLA_HEADS * GLA_QK, GLA_HEADS * GLA_QK, GLA_HEADS * GLA_V, GLA_HEADS * GLA_V,
            GLA_RANK, GLA_RANK)
C_IN = sum(C_SPLITS)
C_OUT = GLA_HEADS * GLA_V

kernel_name = "hybrid_diffattn_retention_gla_macaron"


def _split(t, widths):
    idx = np.cumsum(widths)[:-1].tolist()
    return jnp.split(t, idx, axis=-1)


def _heads(t, n):
    b, s, _ = t.shape
    return t.reshape(b, s, n, -1).transpose(0, 2, 1, 3)


def _merge_heads(t):
    b, n, s, d = t.shape
    return t.transpose(0, 2, 1, 3).reshape(b, s, n * d)


def rmsnorm(x, g):
    xf = x.astype(jnp.float32)
    y = xf * lax.rsqrt(jnp.mean(xf * xf, axis=-1, keepdims=True) + EPS)
    return (y * g.astype(jnp.float32)).astype(x.dtype)


def rope_tables(positions, dim, theta):
    inv = 1.0 / (theta ** (jnp.arange(0, dim, 2, dtype=jnp.float32) / dim))
    ang = positions.astype(jnp.float32)[:, None] * inv[None, :]
    return jnp.cos(ang), jnp.sin(ang)


def apply_rope(x, cos, sin):
    half = x.shape[-1] // 2
    x1, x2 = x[..., :half], x[..., half:]
    c, s = cos.astype(x.dtype), sin.astype(x.dtype)
    return jnp.concatenate([x1 * c - x2 * s, x1 * s + x2 * c], axis=-1)


def apply_partial_rope(x, cos, sin):
    r = x.shape[-1] // ROPE_FRAC
    return jnp.concatenate([apply_rope(x[..., :r], cos, sin), x[..., r:]], axis=-1)


def swiglu(x, w_gate, w_up, w_down):
    return (jax.nn.silu(x @ w_gate) * (x @ w_up)) @ w_down


def diff_attention(q, k, v, lam):
    b, h, _, s, d = q.shape
    nb = s // Q_BLOCK
    q = q * (d ** -0.5)
    qb = q.reshape(b, h, 2, nb, Q_BLOCK, d).transpose(3, 0, 1, 2, 4, 5)

    def block(qblk):
        sc = jnp.einsum('bhcqd,bhckd->bhcqk', qblk, k).astype(jnp.float32)
        p = jax.nn.softmax(sc, axis=-1)
        a = p[:, :, 0] - lam * p[:, :, 1]
        return jnp.einsum('bhqk,bhkv->bhqv', a.astype(v.dtype), v)

    o = lax.map(block, qb)
    return o.transpose(1, 2, 0, 3, 4).reshape(b, h, s, v.shape[-1])


def retention_dir(q, k, v, log_gamma):
    b, h, s, dk = q.shape
    dv = v.shape[-1]
    c = RET_CHUNK
    n = s // c
    qc = q.reshape(b, h, n, c, dk)
    kc = k.reshape(b, h, n, c, dk)
    vc = v.reshape(b, h, n, c, dv)
    idx = jnp.arange(c, dtype=jnp.float32)
    lg = log_gamma.astype(jnp.float32)
    diff = idx[:, None] - idx[None, :]
    intra_decay = jnp.where(diff >= 0, jnp.exp(lg[:, None, None] * jnp.maximum(diff, 0.0)[None]), 0.0)
    sc = jnp.einsum('bhncd,bhnjd->bhncj', qc, kc) * intra_decay[None, :, None]
    o_intra = jnp.einsum('bhncj,bhnjv->bhncv', sc, vc)
    k_dec = jnp.exp(lg[:, None] * (c - 1.0 - idx)[None, :])
    q_dec = jnp.exp(lg[:, None] * (idx + 1.0)[None, :])
    kv = jnp.einsum('bhnjd,bhnjv->nbhdv', kc * k_dec[None, :, None, :, None], vc)
    chunk_decay = jnp.exp(lg * c)[None, :, None, None]

    def step(state, kv_n):
        return chunk_decay * state + kv_n, state

    _, states = lax.scan(step, jnp.zeros(kv.shape[1:], kv.dtype), kv)
    o_inter = jnp.einsum('bhncd,nbhdv->bhncv', qc * q_dec[None, :, None, :, None], states)
    return (o_intra + o_inter).reshape(b, h, s, dv).astype(v.dtype)


def gla_dir(q, k, v, log_a):
    b, h, s, dk = q.shape
    dv = v.shape[-1]
    c = GLA_CHUNK
    n = s // c
    qc = q.reshape(b, h, n, c, dk).astype(jnp.float32)
    kc = k.reshape(b, h, n, c, dk).astype(jnp.float32)
    vc = v.reshape(b, h, n, c, dv).astype(jnp.float32)
    cum = jnp.cumsum(log_a.reshape(b, h, n, c, dk), axis=3)
    last = cum[:, :, :, -1:]
    q_g = qc * jnp.exp(cum)
    sc = jnp.einsum('bhncd,bhnjd->bhncj', q_g, kc * jnp.exp(-cum))
    mask = jnp.tril(jnp.ones((c, c), dtype=bool))
    o_intra = jnp.einsum('bhncj,bhnjv->bhncv', jnp.where(mask, sc, 0.0), vc)
    kv = jnp.einsum('bhnjd,bhnjv->nbhdv', kc * jnp.exp(last - cum), vc)
    dec = jnp.exp(last[:, :, :, 0]).transpose(2, 0, 1, 3)[..., None]

    def step(state, inp):
        kv_n, dec_n = inp
        return dec_n * state + kv_n, state

    _, states = lax.scan(step, jnp.zeros(kv.shape[1:], jnp.float32), (kv, dec))
    o_inter = jnp.einsum('bhncd,nbhdv->bhncv', q_g, states)
    return (o_intra + o_inter).reshape(b, h, s, dv).astype(v.dtype)


def _flip(t):
    return jnp.flip(t, axis=2)


def mixer_ab(h, w_in, lq1, lk1, lq2, lk2, da_norm, ret_logit_f, ret_logit_b, ret_norm, w_out,
             cos_a, sin_a, cos_r, sin_r, lam_init):
    b, s, _ = h.shape
    qa, ka, va, qr, kr, vr, gr = _split(h @ w_in, AB_SPLITS)
    qa = qa.reshape(b, s, DA_HEADS, 2, DA_DIM).transpose(0, 2, 3, 1, 4)
    ka = ka.reshape(b, s, DA_HEADS, 2, DA_DIM).transpose(0, 2, 3, 1, 4)
    qa = apply_partial_rope(qa, cos_a, sin_a)
    ka = apply_partial_rope(ka, cos_a, sin_a)
    va = _heads(va, DA_HEADS)
    lam = (jnp.exp(jnp.sum(lq1.astype(jnp.float32) * lk1.astype(jnp.float32)))
           - jnp.exp(jnp.sum(lq2.astype(jnp.float32) * lk2.astype(jnp.float32))) + lam_init)
    oa = diff_attention(qa, ka, va, lam)
    oa = rmsnorm(oa, da_norm) * (1.0 - lam_init)
    qr = apply_rope(_heads(qr, RET_HEADS), cos_r, sin_r)
    kr = apply_rope(_heads(kr, RET_HEADS), cos_r, sin_r) * (RET_QK ** -0.5)
    vr = _heads(vr, RET_HEADS)
    lg_f = jax.nn.log_sigmoid(ret_logit_f.astype(jnp.float32))
    lg_b = jax.nn.log_sigmoid(ret_logit_b.astype(jnp.float32))
    orr = retention_dir(qr, kr, vr, lg_f) + _flip(retention_dir(_flip(qr), _flip(kr), _flip(vr), lg_b))
    orr = _merge_heads(rmsnorm(orr, ret_norm)) * jax.nn.silu(gr)
    return jnp.concatenate([_merge_heads(oa), orr], axis=-1) @ w_out


def mixer_c(h, w_in, w2_f, b_f, w2_b, b_b, gla_norm, w_out):
    q, k, v, g, lr_f, lr_b = _split(h @ w_in, C_SPLITS)
    q = _heads(q, GLA_HEADS) * (GLA_QK ** -0.5)
    k = _heads(k, GLA_HEADS)
    v = _heads(v, GLA_HEADS)
    la_f = _heads(jax.nn.log_sigmoid((lr_f @ w2_f + b_f).astype(jnp.float32)) / GLA_TAU, GLA_HEADS)
    la_b = _heads(jax.nn.log_sigmoid((lr_b @ w2_b + b_b).astype(jnp.float32)) / GLA_TAU, GLA_HEADS)
    o = gla_dir(q, k, v, la_f) + _flip(gla_dir(_flip(q), _flip(k), _flip(v), _flip(la_b)))
    o = _merge_heads(rmsnorm(o, gla_norm)) * jax.nn.silu(g)
    return o @ w_out


def setup_inputs(seed: int = 0) -> dict:
    key = jax.random.key(seed)
    ks = iter(jax.random.split(key, 40))

    def nrm(shape, scale):
        return jax.random.normal(next(ks), shape, jnp.float32) * scale

    def gain(shape):
        return 1.0 + nrm(shape, 0.02)

    gammas = 1.0 - 2.0 ** (-5.0 - np.arange(RET_HEADS, dtype=np.float64))
    ret_base = jnp.asarray(np.log(gammas / (1.0 - gammas)).astype(np.float32))
    return {
        "x": nrm((BATCH, SEQ, D_MODEL), 1.0),
        "positions": jnp.arange(SEQ, dtype=jnp.int32),
        "ffn1_norm": gain((DEPTH, D_MODEL)),
        "ffn1_w_gate": nrm((DEPTH, D_MODEL, D_FF), D_MODEL ** -0.5),
        "ffn1_w_up": nrm((DEPTH, D_MODEL, D_FF), D_MODEL ** -0.5),
        "ffn1_w_down": nrm((DEPTH, D_FF, D_MODEL), D_FF ** -0.5),
        "ffn2_norm": gain((DEPTH, D_MODEL)),
        "ffn2_w_gate": nrm((DEPTH, D_MODEL, D_FF), D_MODEL ** -0.5),
        "ffn2_w_up": nrm((DEPTH, D_MODEL, D_FF), D_MODEL ** -0.5),
        "ffn2_w_down": nrm((DEPTH, D_FF, D_MODEL), D_FF ** -0.5),
        "ab_norm": gain((N_EVEN, D_MODEL)),
        "ab_w_in": nrm((N_EVEN, D_MODEL, AB_IN), D_MODEL ** -0.5),
        "da_lq1": nrm((N_EVEN, DA_DIM), 0.1),
        "da_lk1": nrm((N_EVEN, DA_DIM), 0.1),
        "da_lq2": nrm((N_EVEN, DA_DIM), 0.1),
        "da_lk2": nrm((N_EVEN, DA_DIM), 0.1),
        "da_norm": gain((N_EVEN, DA_VDIM)),
        "ret_logit_f": ret_base[None, :] + nrm((N_EVEN, RET_HEADS), 0.1),
        "ret_logit_b": ret_base[None, :] + nrm((N_EVEN, RET_HEADS), 0.1),
        "ret_norm": gain((N_EVEN, RET_V)),
        "ab_w_out": nrm((N_EVEN, AB_OUT, D_MODEL), AB_OUT ** -0.5),
        "c_norm": gain((N_ODD, D_MODEL)),
        "c_w_in": nrm((N_ODD, D_MODEL, C_IN), D_MODEL ** -0.5),
        "gla_w2_f": nrm((N_ODD, GLA_RANK, GLA_HEADS * GLA_QK), GLA_RANK ** -0.5),
        "gla_b_f": nrm((N_ODD, GLA_HEADS * GLA_QK), 0.1),
        "gla_w2_b": nrm((N_ODD, GLA_RANK, GLA_HEADS * GLA_QK), GLA_RANK ** -0.5),
        "gla_b_b": nrm((N_ODD, GLA_HEADS * GLA_QK), 0.1),
        "gla_norm": gain((N_ODD, GLA_V)),
        "c_w_out": nrm((N_ODD, C_OUT, D_MODEL), C_OUT ** -0.5),
        "final_norm": gain((D_MODEL,)),
    }


def reference(x, positions, ffn1_norm, ffn1_w_gate, ffn1_w_up, ffn1_w_down,
              ffn2_norm, ffn2_w_gate, ffn2_w_up, ffn2_w_down,
              ab_norm, ab_w_in, da_lq1, da_lk1, da_lq2, da_lk2, da_norm,
              ret_logit_f, ret_logit_b, ret_norm, ab_w_out,
              c_norm, c_w_in, gla_w2_f, gla_b_f, gla_w2_b, gla_b_b, gla_norm, c_w_out,
              final_norm):
    cos_a, sin_a = rope_tables(positions, DA_DIM // ROPE_FRAC, ROPE_THETA)
    cos_r, sin_r = rope_tables(positions, RET_QK, RET_THETA)
    for layer in range(DEPTH):
        i = layer // 2
        x = x + 0.5 * swiglu(rmsnorm(x, ffn1_norm[layer]), ffn1_w_gate[layer], ffn1_w_up[layer], ffn1_w_down[layer])
        if layer % 2 == 0:
            lam_init = 0.8 - 0.6 * math.exp(-0.3 * layer)
            x = x + mixer_ab(rmsnorm(x, ab_norm[i]), ab_w_in[i], da_lq1[i], da_lk1[i], da_lq2[i], da_lk2[i],
                             da_norm[i], ret_logit_f[i], ret_logit_b[i], ret_norm[i], ab_w_out[i],
                             cos_a, sin_a, cos_r, sin_r, lam_init)
        else:
            x = x + mixer_c(rmsnorm(x, c_norm[i]), c_w_in[i], gla_w2_f[i], gla_b_f[i], gla_w2_b[i], gla_b_b[i],
                            gla_norm[i], c_w_out[i])
        x = x + 0.5 * swiglu(rmsnorm(x, ffn2_norm[layer]), ffn2_w_gate[layer], ffn2_w_up[layer], ffn2_w_down[layer])
    return rmsnorm(x, final_norm)
```

```python
import functools
import math

import jax
import jax.numpy as jnp
import numpy as np
from jax import lax
from jax.experimental import pallas as pl
from jax.experimental.pallas import tpu as pltpu

D_MODEL = 1024
D_FF = 2816
DEPTH = 2
DA_HEADS = 4
DA_DIM = 64
DA_VDIM = 2 * DA_DIM
RET_HEADS = 4
RET_QK = 64
RET_V = 128
RET_CHUNK = 128
RET_THETA = 10000.0
GLA_HEADS = 4
GLA_QK = 128
GLA_V = 256
GLA_RANK = 16
GLA_TAU = 16.0
GLA_CHUNK = 64
ROPE_THETA = 500000.0
ROPE_FRAC = 4
EPS = 1e-6

LANES = 128
V7X_VMEM_LIMIT_BYTES = 56 * 1024 * 1024

F32 = jnp.float32
BF16 = jnp.bfloat16
NT_DIMS = (((1,), (1,)), ((), ()))
TN_DIMS = (((0,), (0,)), ((), ()))


def _params(n_grid_dims):
    return pltpu.CompilerParams(
        dimension_semantics=("arbitrary",) * n_grid_dims,
        vmem_limit_bytes=V7X_VMEM_LIMIT_BYTES,
    )


def _resident(shape, index_map=None):
    if index_map is None:
        index_map = lambda *_: (0,) * len(shape)
    return pl.BlockSpec(shape, index_map, pipeline_mode=pl.Buffered(1))


def _rms(x, g):
    return x * lax.rsqrt(jnp.mean(x * x, axis=-1, keepdims=True) + EPS) * g


def _silu(x):
    return x * jax.nn.sigmoid(x)


def _log_sigmoid(x):
    return jnp.minimum(x, 0.0) - jnp.log1p(jnp.exp(-jnp.abs(x)))


def _dot(a, b):
    return jnp.dot(a, b, preferred_element_type=F32)


def _rope_table_kernel(pos_ref, inv_a_ref, inv_r_ref, ca_ref, sa_up_ref, sa_dn_ref,
                       cr_ref, sr_up_ref, sr_dn_ref):
    pos = pos_ref[...]
    lane = lax.broadcasted_iota(jnp.int32, (pos.shape[0], LANES), 1)
    rot = DA_DIM // ROPE_FRAC
    j = lane % DA_DIM
    ang = pos * inv_a_ref[...]
    c, s = jnp.cos(ang), jnp.sin(ang)
    ca_ref[...] = jnp.where(j < rot, c, 1.0)
    sa_up_ref[...] = jnp.where(j < rot // 2, -s, 0.0)
    sa_dn_ref[...] = jnp.where((j >= rot // 2) & (j < rot), s, 0.0)
    j = lane % RET_QK
    ang = pos * inv_r_ref[...]
    c, s = jnp.cos(ang), jnp.sin(ang)
    cr_ref[...] = c
    sr_up_ref[...] = jnp.where(j < RET_QK // 2, -s, 0.0)
    sr_dn_ref[...] = jnp.where(j >= RET_QK // 2, s, 0.0)


def _rope_tables(positions):
    seq = positions.shape[0]
    ts = 512
    rot = DA_DIM // ROPE_FRAC
    inv_a = 1.0 / (ROPE_THETA ** (jnp.arange(0, rot, 2, dtype=F32) / rot))
    inv_r = 1.0 / (RET_THETA ** (jnp.arange(0, RET_QK, 2, dtype=F32) / RET_QK))
    lane = np.arange(LANES)
    inv_a_lanes = inv_a[(lane % DA_DIM) % (rot // 2)][None, :]
    inv_r_lanes = inv_r[(lane % RET_QK) % (RET_QK // 2)][None, :]
    pos = positions.astype(F32).reshape(seq, 1)
    row = pl.BlockSpec((ts, LANES), lambda i: (i, 0))
    vec = pl.BlockSpec((1, LANES), lambda i: (0, 0))
    return pl.pallas_call(
        _rope_table_kernel,
        grid=(seq // ts,),
        in_specs=[pl.BlockSpec((ts, 1), lambda i: (i, 0)), vec, vec],
        out_specs=[row] * 6,
        out_shape=[jax.ShapeDtypeStruct((seq, LANES), F32)] * 6,
        compiler_params=_params(1),
        name="rope_tables",
    )(pos, inv_a_lanes, inv_r_lanes)


def _rope(x, c, s_up, s_dn, half):
    out = []
    for b in range(x.shape[1] // LANES):
        xb = x[:, b * LANES:(b + 1) * LANES]
        up = pltpu.roll(xb, LANES - half, 1)
        dn = pltpu.roll(xb, half, 1)
        out.append(xb * c + up * s_up + dn * s_dn)
    return jnp.concatenate(out, axis=1)


def _ffn_kernel(x_ref, g_ref, wg_ref, wu_ref, wd_ref, fg_ref, o_ref, *, final_norm):
    x = x_ref[...]
    xn = _rms(x, g_ref[...]).astype(BF16)
    gate = _dot(xn, wg_ref[...])
    up = _dot(xn, wu_ref[...])
    h = (_silu(gate) * up).astype(BF16)
    out = x + 0.5 * _dot(h, wd_ref[...])
    if final_norm:
        out = _rms(out, fg_ref[...])
    o_ref[...] = out


def _ffn(x, g, wg, wu, wd, fg, *, final_norm):
    t, d = x.shape
    tm = 512
    tok = pl.BlockSpec((tm, d), lambda i: (i, 0))
    return pl.pallas_call(
        functools.partial(_ffn_kernel, final_norm=final_norm),
        grid=(t // tm,),
        in_specs=[tok, _resident((1, d)), _resident((d, D_FF)), _resident((d, D_FF)),
                  _resident((D_FF, d)), _resident((1, d))],
        out_specs=tok,
        out_shape=jax.ShapeDtypeStruct((t, d), F32),
        compiler_params=_params(1),
        name="ffn",
    )(x, g.reshape(1, d), wg.astype(BF16), wu.astype(BF16), wd.astype(BF16), fg.reshape(1, d))


def _proj_ab_kernel(x_ref, g_ref, w_ref, ca_ref, sa_up_ref, sa_dn_ref, cr_ref, sr_up_ref, sr_dn_ref,
                    qa_ref, ka_ref, va_ref, qr_ref, kr_ref, vr_ref, gr_ref):
    xn = _rms(x_ref[...], g_ref[...]).astype(BF16)
    y = _dot(xn, w_ref[...])
    qa_w = DA_HEADS * 2 * DA_DIM
    va_w = DA_HEADS * DA_VDIM
    qr_w = RET_HEADS * RET_QK
    vr_w = RET_HEADS * RET_V
    o = 0
    qa, o = y[:, o:o + qa_w], o + qa_w
    ka, o = y[:, o:o + qa_w], o + qa_w
    va, o = y[:, o:o + va_w], o + va_w
    qr, o = y[:, o:o + qr_w], o + qr_w
    kr, o = y[:, o:o + qr_w], o + qr_w
    vr, o = y[:, o:o + vr_w], o + vr_w
    gr = y[:, o:o + vr_w]
    rope_a = functools.partial(_rope, c=ca_ref[...], s_up=sa_up_ref[...], s_dn=sa_dn_ref[...],
                               half=DA_DIM // ROPE_FRAC // 2)
    rope_r = functools.partial(_rope, c=cr_ref[...], s_up=sr_up_ref[...], s_dn=sr_dn_ref[...],
                               half=RET_QK // 2)
    qa_ref[...] = (rope_a(qa) * DA_DIM ** -0.5).astype(BF16)
    ka_ref[...] = rope_a(ka).astype(BF16)
    va_ref[...] = va.astype(BF16)
    qr_ref[...] = rope_r(qr)
    kr_ref[...] = rope_r(kr) * RET_QK ** -0.5
    vr_ref[...] = vr.astype(BF16)
    gr_ref[...] = gr


def _proj_ab(x, g, w, tables, seq):
    t, d = x.shape
    tm = 512
    n_in = w.shape[1]
    widths = (DA_HEADS * 2 * DA_DIM, DA_HEADS * 2 * DA_DIM, DA_HEADS * DA_VDIM,
              RET_HEADS * RET_QK, RET_HEADS * RET_QK, RET_HEADS * RET_V, RET_HEADS * RET_V)
    dtypes = (BF16, BF16, BF16, F32, F32, BF16, F32)
    tok = lambda wd: pl.BlockSpec((tm, wd), lambda i: (i, 0))
    tab = pl.BlockSpec((tm, LANES), lambda i: (i % (seq // tm), 0))
    return pl.pallas_call(
        _proj_ab_kernel,
        grid=(t // tm,),
        in_specs=[tok(d), _resident((1, d)), _resident((d, n_in))] + [tab] * 6,
        out_specs=[tok(wd) for wd in widths],
        out_shape=[jax.ShapeDtypeStruct((t, wd), dt) for wd, dt in zip(widths, dtypes)],
        compiler_params=_params(1),
        name="proj_ab",
    )(x, g.reshape(1, d), w.astype(BF16), *tables)


def _proj_c_kernel(x_ref, g_ref, w_ref, q_ref, k_ref, v_ref, gate_ref, lr_ref):
    xn = _rms(x_ref[...], g_ref[...]).astype(BF16)
    y = _dot(xn, w_ref[...])
    qk_w = GLA_HEADS * GLA_QK
    v_w = GLA_HEADS * GLA_V
    q_ref[...] = y[:, :qk_w] * GLA_QK ** -0.5
    k_ref[...] = y[:, qk_w:2 * qk_w]
    v_ref[...] = y[:, 2 * qk_w:2 * qk_w + v_w].astype(BF16)
    gate_ref[...] = y[:, 2 * qk_w + v_w:2 * qk_w + 2 * v_w]
    lr_ref[...] = y[:, 2 * qk_w + 2 * v_w:]


def _proj_c(x, g, w):
    t, d = x.shape
    tm = 512
    n_in = w.shape[1]
    w = jnp.pad(w, ((0, 0), (0, -n_in % LANES)))
    qk_w = GLA_HEADS * GLA_QK
    v_w = GLA_HEADS * GLA_V
    widths = (qk_w, qk_w, v_w, v_w, LANES)
    dtypes = (F32, F32, BF16, F32, F32)
    tok = lambda wd: pl.BlockSpec((tm, wd), lambda i: (i, 0))
    return pl.pallas_call(
        _proj_c_kernel,
        grid=(t // tm,),
        in_specs=[tok(d), _resident((1, d)), _resident((d, w.shape[1]))],
        out_specs=[tok(wd) for wd in widths],
        out_shape=[jax.ShapeDtypeStruct((t, wd), dt) for wd, dt in zip(widths, dtypes)],
        compiler_params=_params(1),
        name="proj_c",
    )(x, g.reshape(1, d), w.astype(BF16))


def _out_proj_kernel(*refs):
    x_ref, o_ref = refs[0], refs[-1]
    acc = x_ref[...]
    for a_ref, w_ref in zip(refs[1:-1:2], refs[2:-1:2]):
        acc = acc + _dot(a_ref[...], w_ref[...])
    o_ref[...] = acc


def _out_proj(x, acts, w):
    t, d = x.shape
    tm = 1024
    tok = lambda wd: pl.BlockSpec((tm, wd), lambda i: (i, 0))
    w = w.astype(BF16)
    in_specs, args, row = [tok(d)], [x], 0
    for a in acts:
        wd = a.shape[1]
        in_specs += [tok(wd), _resident((wd, d), functools.partial(lambda r, i: (r, 0), row // wd))]
        args += [a, w]
        row += wd
    return pl.pallas_call(
        _out_proj_kernel,
        grid=(t // tm,),
        in_specs=in_specs,
        out_specs=tok(d),
        out_shape=jax.ShapeDtypeStruct((t, d), F32),
        compiler_params=_params(1),
        name="out_proj",
    )(*args)


def _diff_attn_kernel(q_ref, k_ref, v_ref, lq1_ref, lk1_ref, lq2_ref, lk2_ref, g_ref, o_ref, *, lam_init):
    q, k, v = q_ref[0], k_ref[0], v_ref[0]
    lane = lax.broadcasted_iota(jnp.int32, q.shape, 1)
    zero = jnp.zeros_like(q)
    s1 = lax.dot_general(jnp.where(lane < DA_DIM, q, zero), k, NT_DIMS, preferred_element_type=F32)
    s2 = lax.dot_general(jnp.where(lane >= DA_DIM, q, zero), k, NT_DIMS, preferred_element_type=F32)
    p1 = jnp.exp(s1 - jnp.max(s1, axis=-1, keepdims=True))
    p2 = jnp.exp(s2 - jnp.max(s2, axis=-1, keepdims=True))
    lam = (jnp.exp(jnp.sum(lq1_ref[...] * lk1_ref[...], axis=-1, keepdims=True))
           - jnp.exp(jnp.sum(lq2_ref[...] * lk2_ref[...], axis=-1, keepdims=True)) + lam_init)
    c1 = 1.0 / jnp.sum(p1, axis=-1, keepdims=True)
    c2 = lam / jnp.sum(p2, axis=-1, keepdims=True)
    a = (p1 * c1 - p2 * c2).astype(BF16)
    o = _dot(a, v)
    o_ref[0] = (_rms(o, g_ref[...]) * (1.0 - lam_init)).astype(BF16)


def _diff_attn(q, k, v, lq1, lk1, lq2, lk2, g, lam_init):
    b, s, _ = q.shape
    tq = 256
    hw = 2 * DA_DIM
    vec = lambda n: _resident((1, n))
    return pl.pallas_call(
        functools.partial(_diff_attn_kernel, lam_init=lam_init),
        grid=(b, DA_HEADS, s // tq),
        in_specs=[pl.BlockSpec((1, tq, hw), lambda bi, h, qi: (bi, qi, h)),
                  pl.BlockSpec((1, s, hw), lambda bi, h, qi: (bi, 0, h)),
                  pl.BlockSpec((1, s, DA_VDIM), lambda bi, h, qi: (bi, 0, h)),
                  vec(DA_DIM), vec(DA_DIM), vec(DA_DIM), vec(DA_DIM), vec(DA_VDIM)],
        out_specs=pl.BlockSpec((1, tq, DA_VDIM), lambda bi, h, qi: (bi, qi, h)),
        out_shape=jax.ShapeDtypeStruct((b, s, DA_HEADS * DA_VDIM), BF16),
        compiler_params=_params(3),
        name="diff_attn",
    )(q, k, v, lq1.reshape(1, -1), lk1.reshape(1, -1), lq2.reshape(1, -1), lk2.reshape(1, -1),
      g.reshape(1, -1))


RET_PAIR = LANES // RET_QK


def _retention_kernel(q_ref, k_ref, v_ref, gate_ref, lf_ref, lb_ref, g_ref, o_ref, sf_ref, sb_ref):
    c = RET_CHUNK
    n_chunks = q_ref.shape[1] // c
    row = lax.broadcasted_iota(jnp.int32, (c, c), 0).astype(F32)
    col = lax.broadcasted_iota(jnp.int32, (c, c), 1).astype(F32)
    lane = lax.broadcasted_iota(jnp.int32, (c, LANES), 1)
    rows = lambda n: pl.ds(pl.multiple_of(n * c, c), c)
    for j in range(RET_PAIR):
        vcols = slice(j * RET_V, (j + 1) * RET_V)
        lg_f = _log_sigmoid(lf_ref[j])
        lg_b = _log_sigmoid(lb_ref[j])
        diff = row - col
        decay = (jnp.where(diff >= 0, jnp.exp(lg_f * jnp.maximum(diff, 0.0)), 0.0)
                 + jnp.where(diff <= 0, jnp.exp(lg_b * jnp.maximum(-diff, 0.0)), 0.0))
        q_dec_f = jnp.exp(lg_f * (row + 1.0))
        k_dec_f = jnp.exp(lg_f * (c - 1.0 - row))
        q_dec_b = jnp.exp(lg_b * (c - row))
        k_dec_b = jnp.exp(lg_b * row)
        chunk_dec_f = jnp.exp(lg_f * c)
        chunk_dec_b = jnp.exp(lg_b * c)
        head = (lane // RET_QK) == j

        def fwd(n, state):
            sf_ref[n] = state.astype(BF16)
            kd = (k_ref[0, rows(n), :] * k_dec_f).astype(BF16)
            kv = lax.dot_general(kd, v_ref[0, rows(n), vcols], TN_DIMS, preferred_element_type=F32)
            return chunk_dec_f * state + kv

        def bwd(i, state):
            n = n_chunks - 1 - i
            sb_ref[n] = state.astype(BF16)
            kd = (k_ref[0, rows(n), :] * k_dec_b).astype(BF16)
            kv = lax.dot_general(kd, v_ref[0, rows(n), vcols], TN_DIMS, preferred_element_type=F32)
            return chunk_dec_b * state + kv

        zero = jnp.zeros((LANES, RET_V), F32)
        lax.fori_loop(0, n_chunks, fwd, zero)
        lax.fori_loop(0, n_chunks, bwd, zero)

        def out(n, carry):
            qc = jnp.where(head, q_ref[0, rows(n), :], 0.0)
            sc = lax.dot_general(qc.astype(BF16), k_ref[0, rows(n), :].astype(BF16), NT_DIMS,
                                 preferred_element_type=F32) * decay
            o = (_dot(sc.astype(BF16), v_ref[0, rows(n), vcols])
                 + _dot((qc * q_dec_f).astype(BF16), sf_ref[n])
                 + _dot((qc * q_dec_b).astype(BF16), sb_ref[n]))
            o = _rms(o, g_ref[...]) * _silu(gate_ref[0, rows(n), vcols])
            o_ref[0, rows(n), vcols] = o.astype(BF16)
            return carry

        lax.fori_loop(0, n_chunks, out, 0)


def _retention(q, k, v, gate, logit_f, logit_b, g):
    b, s, _ = q.shape
    n_chunks = s // RET_CHUNK
    pw = RET_PAIR * RET_V
    lanes = lambda x: jnp.broadcast_to(x.astype(F32)[:, None, None], (RET_HEADS, 1, LANES))
    qk = pl.BlockSpec((1, s, LANES), lambda bi, p: (bi, 0, p))
    vg = pl.BlockSpec((1, s, pw), lambda bi, p: (bi, 0, p))
    logit = pl.BlockSpec((RET_PAIR, 1, LANES), lambda bi, p: (p, 0, 0))
    return pl.pallas_call(
        _retention_kernel,
        grid=(b, RET_HEADS // RET_PAIR),
        in_specs=[qk, qk, vg, vg, logit, logit, _resident((1, RET_V))],
        out_specs=vg,
        out_shape=jax.ShapeDtypeStruct((b, s, RET_HEADS * RET_V), BF16),
        scratch_shapes=[pltpu.VMEM((n_chunks, LANES, RET_V), BF16),
                        pltpu.VMEM((n_chunks, LANES, RET_V), BF16)],
        compiler_params=_params(2),
        name="retention",
    )(q, k, v, gate, lanes(logit_f), lanes(logit_b), g.reshape(1, -1))


def _chunk_cumsum(x, reverse):
    n = x.shape[0]
    pos = lax.broadcasted_iota(jnp.int32, x.shape, 0) % GLA_CHUNK
    step = 1
    while step < GLA_CHUNK:
        if reverse:
            x = x + jnp.where(pos < GLA_CHUNK - step, pltpu.roll(x, n - step, 0), 0.0)
        else:
            x = x + jnp.where(pos >= step, pltpu.roll(x, step, 0), 0.0)
        step *= 2
    return x


def _gla_kernel(q_ref, k_ref, v_ref, gate_ref, lr_ref, w2f_ref, w2b_ref, bf_ref, bb_ref, g_ref, o_ref,
                cum_ref, acc_ref):
    c = GLA_CHUNK
    n_chunks = q_ref.shape[1] // c
    lr = lr_ref[0].astype(BF16)
    rows = lambda n: pl.ds(pl.multiple_of(n * c, c), c)
    row = lax.broadcasted_iota(jnp.int32, (c, c), 0)
    col = lax.broadcasted_iota(jnp.int32, (c, c), 1)

    def direction(w2_ref, b_ref, reverse):
        log_a = _log_sigmoid(_dot(lr, w2_ref[...]) + b_ref[...]) / GLA_TAU
        cum_ref[...] = _chunk_cumsum(log_a, reverse)
        keep = (row <= col) if reverse else (row >= col)
        last_row = 0 if reverse else c - 1

        def chunk(i, state):
            n = n_chunks - 1 - i if reverse else i
            cum = cum_ref[rows(n), :]
            last = cum[last_row:last_row + 1, :]
            kc = k_ref[0, rows(n), :]
            vc = v_ref[0, rows(n), :]
            qg = (q_ref[0, rows(n), :] * jnp.exp(cum)).astype(BF16)
            sc = lax.dot_general(qg, (kc * jnp.exp(-cum)).astype(BF16), NT_DIMS, preferred_element_type=F32)
            o = (_dot(jnp.where(keep, sc, 0.0).astype(BF16), vc)
                 + lax.dot_general(qg, state.astype(BF16), NT_DIMS, preferred_element_type=F32))
            if reverse:
                acc_ref[rows(n), :] += o
            else:
                acc_ref[rows(n), :] = o
            kv = lax.dot_general(vc, (kc * jnp.exp(last - cum)).astype(BF16), TN_DIMS,
                                 preferred_element_type=F32)
            return jnp.exp(last) * state + kv

        lax.fori_loop(0, n_chunks, chunk, jnp.zeros((GLA_V, GLA_QK), F32))

    direction(w2f_ref, bf_ref, reverse=False)
    direction(w2b_ref, bb_ref, reverse=True)
    o_ref[0] = (_rms(acc_ref[...], g_ref[...]) * _silu(gate_ref[0])).astype(BF16)


def _gla(q, k, v, gate, lr, w2_f, b_f, w2_b, b_b, g):
    b, s, _ = q.shape
    hk = GLA_HEADS * GLA_QK
    w2f = jnp.zeros((LANES, hk), BF16).at[:GLA_RANK].set(w2_f.astype(BF16))
    w2b = jnp.zeros((LANES, hk), BF16).at[GLA_RANK:2 * GLA_RANK].set(w2_b.astype(BF16))
    qk = pl.BlockSpec((1, s, GLA_QK), lambda bi, h: (bi, 0, h))
    vg = pl.BlockSpec((1, s, GLA_V), lambda bi, h: (bi, 0, h))
    w2 = pl.BlockSpec((LANES, GLA_QK), lambda bi, h: (0, h))
    bias = pl.BlockSpec((1, GLA_QK), lambda bi, h: (0, h))
    return pl.pallas_call(
        _gla_kernel,
        grid=(b, GLA_HEADS),
        in_specs=[qk, qk, vg, vg, pl.BlockSpec((1, s, LANES), lambda bi, h: (bi, 0, 0)),
                  w2, w2, bias, bias, _resident((1, GLA_V))],
        out_specs=vg,
        out_shape=jax.ShapeDtypeStruct((b, s, GLA_HEADS * GLA_V), BF16),
        scratch_shapes=[pltpu.VMEM((s, GLA_QK), F32), pltpu.VMEM((s, GLA_V), F32)],
        compiler_params=_params(2),
        name="gla",
    )(q, k, v, gate, lr, w2f, w2b, b_f.reshape(1, -1), b_b.reshape(1, -1), g.reshape(1, -1))


def kernel(x, positions, ffn1_norm, ffn1_w_gate, ffn1_w_up, ffn1_w_down, ffn2_norm, ffn2_w_gate, ffn2_w_up, ffn2_w_down, ab_norm, ab_w_in, da_lq1, da_lk1, da_lq2, da_lk2, da_norm, ret_logit_f, ret_logit_b, ret_norm, ab_w_out, c_norm, c_w_in, gla_w2_f, gla_b_f, gla_w2_b, gla_b_b, gla_norm, c_w_out, final_norm):
    b, s, d = x.shape
    t = b * s
    tables = _rope_tables(positions)
    x = x.reshape(t, d)
    seq3 = lambda a: a.reshape(b, s, -1)
    tok2 = lambda a: a.reshape(t, -1)
    for layer in range(DEPTH):
        i = layer // 2
        x = _ffn(x, ffn1_norm[layer], ffn1_w_gate[layer], ffn1_w_up[layer], ffn1_w_down[layer],
                 final_norm, final_norm=False)
        if layer % 2 == 0:
            lam_init = 0.8 - 0.6 * math.exp(-0.3 * layer)
            qa, ka, va, qr, kr, vr, gr = _proj_ab(x, ab_norm[i], ab_w_in[i], tables, s)
            oa = _diff_attn(seq3(qa), seq3(ka), seq3(va), da_lq1[i], da_lk1[i], da_lq2[i], da_lk2[i],
                            da_norm[i], lam_init)
            orr = _retention(seq3(qr), seq3(kr), seq3(vr), seq3(gr), ret_logit_f[i], ret_logit_b[i],
                             ret_norm[i])
            x = _out_proj(x, [tok2(oa), tok2(orr)], ab_w_out[i])
        else:
            q, k, v, gate, lr = _proj_c(x, c_norm[i], c_w_in[i])
            o = _gla(seq3(q), seq3(k), seq3(v), seq3(gate), seq3(lr), gla_w2_f[i], gla_b_f[i],
                     gla_w2_b[i], gla_b_b[i], gla_norm[i])
            x = _out_proj(x, [tok2(o)], c_w_out[i])
        x = _ffn(x, ffn2_norm[layer], ffn2_w_gate[layer], ffn2_w_up[layer], ffn2_w_down[layer],
                 final_norm, final_norm=(layer == DEPTH - 1))
    return x.reshape(b, s, d)
```

```python
import functools
import math

import jax
import jax.numpy as jnp
import numpy as np
from jax import lax
from jax.experimental import pallas as pl
from jax.experimental.pallas import tpu as pltpu

D_MODEL = 1024
D_FF = 2816
DEPTH = 2
DA_HEADS = 4
DA_DIM = 64
DA_VDIM = 2 * DA_DIM
RET_HEADS = 4
RET_QK = 64
RET_V = 128
RET_CHUNK = 128
RET_THETA = 10000.0
GLA_HEADS = 4
GLA_QK = 128
GLA_V = 256
GLA_RANK = 16
GLA_TAU = 16.0
GLA_CHUNK = 64
ROPE_THETA = 500000.0
ROPE_FRAC = 4
EPS = 1e-6

LANES = 128
V7X_VMEM_LIMIT_BYTES = 56 * 1024 * 1024

F32 = jnp.float32
BF16 = jnp.bfloat16
NT_DIMS = (((1,), (1,)), ((), ()))
TN_DIMS = (((0,), (0,)), ((), ()))


def _params(n_grid_dims):
    return pltpu.CompilerParams(
        dimension_semantics=("arbitrary",) * n_grid_dims,
        vmem_limit_bytes=V7X_VMEM_LIMIT_BYTES,
    )


def _resident(shape, index_map=None):
    if index_map is None:
        index_map = lambda *_: (0,) * len(shape)
    return pl.BlockSpec(shape, index_map, pipeline_mode=pl.Buffered(1))


def _rms(x, g):
    return x * lax.rsqrt(jnp.mean(x * x, axis=-1, keepdims=True) + EPS) * g


def _silu(x):
    return x * jax.nn.sigmoid(x)


def _log_sigmoid(x):
    return jnp.minimum(x, 0.0) - jnp.log(1.0 + jnp.exp(-jnp.abs(x)))


def _dot(a, b):
    return jnp.dot(a, b, preferred_element_type=F32)


def _rope_table_kernel(pos_ref, inv_a_ref, inv_r_ref, ca_ref, sa_up_ref, sa_dn_ref,
                       cr_ref, sr_up_ref, sr_dn_ref):
    pos = pos_ref[...]
    lane = lax.broadcasted_iota(jnp.int32, (pos.shape[0], LANES), 1)
    rot = DA_DIM // ROPE_FRAC
    j = lane % DA_DIM
    ang = pos * inv_a_ref[...]
    c, s = jnp.cos(ang), jnp.sin(ang)
    ca_ref[...] = jnp.where(j < rot, c, 1.0)
    sa_up_ref[...] = jnp.where(j < rot // 2, -s, 0.0)
    sa_dn_ref[...] = jnp.where((j >= rot // 2) & (j < rot), s, 0.0)
    j = lane % RET_QK
    ang = pos * inv_r_ref[...]
    c, s = jnp.cos(ang), jnp.sin(ang)
    cr_ref[...] = c
    sr_up_ref[...] = jnp.where(j < RET_QK // 2, -s, 0.0)
    sr_dn_ref[...] = jnp.where(j >= RET_QK // 2, s, 0.0)


def _rope_tables(positions):
    seq = positions.shape[0]
    ts = 512
    rot = DA_DIM // ROPE_FRAC
    inv_a = 1.0 / (ROPE_THETA ** (jnp.arange(0, rot, 2, dtype=F32) / rot))
    inv_r = 1.0 / (RET_THETA ** (jnp.arange(0, RET_QK, 2, dtype=F32) / RET_QK))
    lane = np.arange(LANES)
    inv_a_lanes = inv_a[(lane % DA_DIM) % (rot // 2)][None, :]
    inv_r_lanes = inv_r[(lane % RET_QK) % (RET_QK // 2)][None, :]
    pos = positions.astype(F32).reshape(seq, 1)
    row = pl.BlockSpec((ts, LANES), lambda i: (i, 0))
    vec = pl.BlockSpec((1, LANES), lambda i: (0, 0))
    return pl.pallas_call(
        _rope_table_kernel,
        grid=(seq // ts,),
        in_specs=[pl.BlockSpec((ts, 1), lambda i: (i, 0)), vec, vec],
        out_specs=[row] * 6,
        out_shape=[jax.ShapeDtypeStruct((seq, LANES), F32)] * 6,
        compiler_params=_params(1),
        name="rope_tables",
    )(pos, inv_a_lanes, inv_r_lanes)


def _rope(x, c, s_up, s_dn, half):
    out = []
    for b in range(x.shape[1] // LANES):
        xb = x[:, b * LANES:(b + 1) * LANES]
        up = pltpu.roll(xb, LANES - half, 1)
        dn = pltpu.roll(xb, half, 1)
        out.append(xb * c + up * s_up + dn * s_dn)
    return jnp.concatenate(out, axis=1)


def _ffn_kernel(x_ref, g_ref, wg_ref, wu_ref, wd_ref, fg_ref, o_ref, *, final_norm):
    x = x_ref[...]
    xn = _rms(x, g_ref[...]).astype(BF16)
    gate = _dot(xn, wg_ref[...])
    up = _dot(xn, wu_ref[...])
    h = (_silu(gate) * up).astype(BF16)
    out = x + 0.5 * _dot(h, wd_ref[...])
    if final_norm:
        out = _rms(out, fg_ref[...])
    o_ref[...] = out


def _ffn(x, g, wg, wu, wd, fg, *, final_norm):
    t, d = x.shape
    tm = 512
    tok = pl.BlockSpec((tm, d), lambda i: (i, 0))
    return pl.pallas_call(
        functools.partial(_ffn_kernel, final_norm=final_norm),
        grid=(t // tm,),
        in_specs=[tok, _resident((1, d)), _resident((d, D_FF)), _resident((d, D_FF)),
                  _resident((D_FF, d)), _resident((1, d))],
        out_specs=tok,
        out_shape=jax.ShapeDtypeStruct((t, d), F32),
        compiler_params=_params(1),
        name="ffn",
    )(x, g.reshape(1, d), wg.astype(BF16), wu.astype(BF16), wd.astype(BF16), fg.reshape(1, d))


def _proj_ab_kernel(x_ref, g_ref, w_ref, ca_ref, sa_up_ref, sa_dn_ref, cr_ref, sr_up_ref, sr_dn_ref,
                    qa_ref, ka_ref, va_ref, qr_ref, kr_ref, vr_ref, gr_ref):
    xn = _rms(x_ref[...], g_ref[...]).astype(BF16)
    y = _dot(xn, w_ref[...])
    qa_w = DA_HEADS * 2 * DA_DIM
    va_w = DA_HEADS * DA_VDIM
    qr_w = RET_HEADS * RET_QK
    vr_w = RET_HEADS * RET_V
    o = 0
    qa, o = y[:, o:o + qa_w], o + qa_w
    ka, o = y[:, o:o + qa_w], o + qa_w
    va, o = y[:, o:o + va_w], o + va_w
    qr, o = y[:, o:o + qr_w], o + qr_w
    kr, o = y[:, o:o + qr_w], o + qr_w
    vr, o = y[:, o:o + vr_w], o + vr_w
    gr = y[:, o:o + vr_w]
    rope_a = functools.partial(_rope, c=ca_ref[...], s_up=sa_up_ref[...], s_dn=sa_dn_ref[...],
                               half=DA_DIM // ROPE_FRAC // 2)
    rope_r = functools.partial(_rope, c=cr_ref[...], s_up=sr_up_ref[...], s_dn=sr_dn_ref[...],
                               half=RET_QK // 2)
    qa_ref[...] = (rope_a(qa) * DA_DIM ** -0.5).astype(BF16)
    ka_ref[...] = rope_a(ka).astype(BF16)
    va_ref[...] = va.astype(BF16)
    qr_ref[...] = rope_r(qr)
    kr_ref[...] = rope_r(kr) * RET_QK ** -0.5
    vr_ref[...] = vr.astype(BF16)
    gr_ref[...] = gr


def _proj_ab(x, g, w, tables, seq):
    t, d = x.shape
    tm = 512
    n_in = w.shape[1]
    widths = (DA_HEADS * 2 * DA_DIM, DA_HEADS * 2 * DA_DIM, DA_HEADS * DA_VDIM,
              RET_HEADS * RET_QK, RET_HEADS * RET_QK, RET_HEADS * RET_V, RET_HEADS * RET_V)
    dtypes = (BF16, BF16, BF16, F32, F32, BF16, F32)
    tok = lambda wd: pl.BlockSpec((tm, wd), lambda i: (i, 0))
    tab = pl.BlockSpec((tm, LANES), lambda i: (i % (seq // tm), 0))
    return pl.pallas_call(
        _proj_ab_kernel,
        grid=(t // tm,),
        in_specs=[tok(d), _resident((1, d)), _resident((d, n_in))] + [tab] * 6,
        out_specs=[tok(wd) for wd in widths],
        out_shape=[jax.ShapeDtypeStruct((t, wd), dt) for wd, dt in zip(widths, dtypes)],
        compiler_params=_params(1),
        name="proj_ab",
    )(x, g.reshape(1, d), w.astype(BF16), *tables)


def _proj_c_kernel(x_ref, g_ref, w_ref, q_ref, k_ref, v_ref, gate_ref, lr_ref):
    xn = _rms(x_ref[...], g_ref[...]).astype(BF16)
    y = _dot(xn, w_ref[...])
    qk_w = GLA_HEADS * GLA_QK
    v_w = GLA_HEADS * GLA_V
    q_ref[...] = y[:, :qk_w] * GLA_QK ** -0.5
    k_ref[...] = y[:, qk_w:2 * qk_w]
    v_ref[...] = y[:, 2 * qk_w:2 * qk_w + v_w].astype(BF16)
    gate_ref[...] = y[:, 2 * qk_w + v_w:2 * qk_w + 2 * v_w]
    lr_ref[...] = y[:, 2 * qk_w + 2 * v_w:]


def _proj_c(x, g, w):
    t, d = x.shape
    tm = 512
    n_in = w.shape[1]
    w = jnp.pad(w, ((0, 0), (0, -n_in % LANES)))
    qk_w = GLA_HEADS * GLA_QK
    v_w = GLA_HEADS * GLA_V
    widths = (qk_w, qk_w, v_w, v_w, LANES)
    dtypes = (F32, F32, BF16, F32, F32)
    tok = lambda wd: pl.BlockSpec((tm, wd), lambda i: (i, 0))
    return pl.pallas_call(
        _proj_c_kernel,
        grid=(t // tm,),
        in_specs=[tok(d), _resident((1, d)), _resident((d, w.shape[1]))],
        out_specs=[tok(wd) for wd in widths],
        out_shape=[jax.ShapeDtypeStruct((t, wd), dt) for wd, dt in zip(widths, dtypes)],
        compiler_params=_params(1),
        name="proj_c",
    )(x, g.reshape(1, d), w.astype(BF16))


def _out_proj_kernel(*refs):
    x_ref, o_ref = refs[0], refs[-1]
    acc = x_ref[...]
    for a_ref, w_ref in zip(refs[1:-1:2], refs[2:-1:2]):
        acc = acc + _dot(a_ref[...], w_ref[...])
    o_ref[...] = acc


def _out_proj(x, acts, w):
    t, d = x.shape
    tm = 1024
    tok = lambda wd: pl.BlockSpec((tm, wd), lambda i: (i, 0))
    w = w.astype(BF16)
    in_specs, args, row = [tok(d)], [x], 0
    for a in acts:
        wd = a.shape[1]
        in_specs += [tok(wd), _resident((wd, d), functools.partial(lambda r, i: (r, 0), row // wd))]
        args += [a, w]
        row += wd
    return pl.pallas_call(
        _out_proj_kernel,
        grid=(t // tm,),
        in_specs=in_specs,
        out_specs=tok(d),
        out_shape=jax.ShapeDtypeStruct((t, d), F32),
        compiler_params=_params(1),
        name="out_proj",
    )(*args)


def _diff_attn_kernel(q_ref, k_ref, v_ref, lq1_ref, lk1_ref, lq2_ref, lk2_ref, g_ref, o_ref, *, lam_init):
    q, k, v = q_ref[0], k_ref[0], v_ref[0]
    lane = lax.broadcasted_iota(jnp.int32, q.shape, 1)
    zero = jnp.zeros_like(q)
    s1 = lax.dot_general(jnp.where(lane < DA_DIM, q, zero), k, NT_DIMS, preferred_element_type=F32)
    s2 = lax.dot_general(jnp.where(lane >= DA_DIM, q, zero), k, NT_DIMS, preferred_element_type=F32)
    p1 = jnp.exp(s1 - jnp.max(s1, axis=-1, keepdims=True))
    p2 = jnp.exp(s2 - jnp.max(s2, axis=-1, keepdims=True))
    lam = (jnp.exp(jnp.sum(lq1_ref[...] * lk1_ref[...], axis=-1, keepdims=True))
           - jnp.exp(jnp.sum(lq2_ref[...] * lk2_ref[...], axis=-1, keepdims=True)) + lam_init)
    c1 = 1.0 / jnp.sum(p1, axis=-1, keepdims=True)
    c2 = lam / jnp.sum(p2, axis=-1, keepdims=True)
    a = (p1 * c1 - p2 * c2).astype(BF16)
    o = _dot(a, v)
    o_ref[0] = (_rms(o, g_ref[...]) * (1.0 - lam_init)).astype(BF16)


def _diff_attn(q, k, v, lq1, lk1, lq2, lk2, g, lam_init):
    b, s, _ = q.shape
    tq = 256
    hw = 2 * DA_DIM
    vec = lambda n: _resident((1, n))
    return pl.pallas_call(
        functools.partial(_diff_attn_kernel, lam_init=lam_init),
        grid=(b, DA_HEADS, s // tq),
        in_specs=[pl.BlockSpec((1, tq, hw), lambda bi, h, qi: (bi, qi, h)),
                  pl.BlockSpec((1, s, hw), lambda bi, h, qi: (bi, 0, h)),
                  pl.BlockSpec((1, s, DA_VDIM), lambda bi, h, qi: (bi, 0, h)),
                  vec(DA_DIM), vec(DA_DIM), vec(DA_DIM), vec(DA_DIM), vec(DA_VDIM)],
        out_specs=pl.BlockSpec((1, tq, DA_VDIM), lambda bi, h, qi: (bi, qi, h)),
        out_shape=jax.ShapeDtypeStruct((b, s, DA_HEADS * DA_VDIM), BF16),
        compiler_params=_params(3),
        name="diff_attn",
    )(q, k, v, lq1.reshape(1, -1), lk1.reshape(1, -1), lq2.reshape(1, -1), lk2.reshape(1, -1),
      g.reshape(1, -1))


RET_PAIR = LANES // RET_QK


def _retention_kernel(q_ref, k_ref, v_ref, gate_ref, lf_ref, lb_ref, g_ref, o_ref, st_ref):
    c = RET_CHUNK
    n_chunks = q_ref.shape[1] // c
    pw = RET_PAIR * RET_V
    row = lax.broadcasted_iota(jnp.int32, (c, c), 0).astype(F32)
    col = lax.broadcasted_iota(jnp.int32, (c, c), 1).astype(F32)
    lane_head = lax.broadcasted_iota(jnp.int32, (c, LANES), 1) // RET_QK
    rows = lambda n: pl.ds(pl.multiple_of(n * c, c), c)
    heads = range(RET_PAIR)
    lg_f = [_log_sigmoid(lf_ref[j]) for j in heads]
    lg_b = [_log_sigmoid(lb_ref[j]) for j in heads]

    def per_lane(lg):
        out = lg[0]
        for j in heads[1:]:
            out = jnp.where(lane_head[:1] == j, lg[j], out)
        return out

    per_col = lambda lg: jnp.concatenate([jnp.broadcast_to(x, (1, RET_V)) for x in lg], axis=1)
    diff = row - col
    decay = [jnp.where(diff >= 0, jnp.exp(lg_f[j] * jnp.maximum(diff, 0.0)), 0.0)
             + jnp.where(diff <= 0, jnp.exp(lg_b[j] * jnp.maximum(-diff, 0.0)), 0.0) for j in heads]
    lgl_f, lgl_b = per_lane(lg_f), per_lane(lg_b)
    q_dec_f = jnp.exp(lgl_f * (row + 1.0))
    k_dec_f = jnp.exp(lgl_f * (c - 1.0 - row))
    q_dec_b = jnp.exp(lgl_b * (c - row))
    k_dec_b = jnp.exp(lgl_b * row)
    chunk_dec_f = jnp.exp(per_col(lg_f) * c)
    chunk_dec_b = jnp.exp(per_col(lg_b) * c)
    own_block = (lax.broadcasted_iota(jnp.int32, (LANES, pw), 0) // RET_QK
                 == lax.broadcasted_iota(jnp.int32, (LANES, pw), 1) // RET_V)

    def kv(n, k_dec):
        kd = (k_ref[0, rows(n), :] * k_dec).astype(BF16)
        out = lax.dot_general(kd, v_ref[0, rows(n), :], TN_DIMS, preferred_element_type=F32)
        return jnp.where(own_block, out, 0.0)

    def scan(i, states):
        sf, sb = states
        nf, nb = i, n_chunks - 1 - i
        st_ref[nf, :LANES, :] = sf.astype(BF16)
        st_ref[nb, LANES:, :] = sb.astype(BF16)
        return chunk_dec_f * sf + kv(nf, k_dec_f), chunk_dec_b * sb + kv(nb, k_dec_b)

    zero = jnp.zeros((LANES, pw), F32)
    lax.fori_loop(0, n_chunks, scan, (zero, zero), unroll=4)

    def out(n, carry):
        qc = q_ref[0, rows(n), :]
        kc = k_ref[0, rows(n), :].astype(BF16)
        vc = v_ref[0, rows(n), :]
        q_heads = jnp.concatenate([jnp.where(lane_head == j, qc, 0.0) for j in heads], axis=0)
        sc = lax.dot_general(q_heads.astype(BF16), kc, NT_DIMS, preferred_element_type=F32)
        inter = _dot(jnp.concatenate([qc * q_dec_f, qc * q_dec_b], axis=1).astype(BF16), st_ref[n])
        o = []
        for j in heads:
            a = (sc[j * c:(j + 1) * c] * decay[j]).astype(BF16)
            oj = _dot(a, vc[:, j * RET_V:(j + 1) * RET_V]) + inter[:, j * RET_V:(j + 1) * RET_V]
            o.append(_rms(oj, g_ref[...]))
        o = jnp.concatenate(o, axis=1) * _silu(gate_ref[0, rows(n), :])
        o_ref[0, rows(n), :] = o.astype(BF16)
        return carry

    lax.fori_loop(0, n_chunks, out, 0, unroll=4)


def _retention(q, k, v, gate, logit_f, logit_b, g):
    b, s, _ = q.shape
    n_chunks = s // RET_CHUNK
    pw = RET_PAIR * RET_V
    lanes = lambda x: jnp.broadcast_to(x.astype(F32)[:, None, None], (RET_HEADS, 1, LANES))
    qk = pl.BlockSpec((1, s, LANES), lambda bi, p: (bi, 0, p))
    vg = pl.BlockSpec((1, s, pw), lambda bi, p: (bi, 0, p))
    logit = pl.BlockSpec((RET_PAIR, 1, LANES), lambda bi, p: (p, 0, 0))
    return pl.pallas_call(
        _retention_kernel,
        grid=(b, RET_HEADS // RET_PAIR),
        in_specs=[qk, qk, vg, vg, logit, logit, _resident((1, RET_V))],
        out_specs=vg,
        out_shape=jax.ShapeDtypeStruct((b, s, RET_HEADS * RET_V), BF16),
        scratch_shapes=[pltpu.VMEM((n_chunks, 2 * LANES, pw), BF16)],
        compiler_params=_params(2),
        name="retention",
    )(q, k, v, gate, lanes(logit_f), lanes(logit_b), g.reshape(1, -1))


def _chunk_cumsum(x, reverse):
    n = x.shape[0]
    pos = lax.broadcasted_iota(jnp.int32, x.shape, 0) % GLA_CHUNK
    step = 1
    while step < GLA_CHUNK:
        if reverse:
            x = x + jnp.where(pos < GLA_CHUNK - step, pltpu.roll(x, n - step, 0), 0.0)
        else:
            x = x + jnp.where(pos >= step, pltpu.roll(x, step, 0), 0.0)
        step *= 2
    return x


def _gla_kernel(q_ref, k_ref, v_ref, gate_ref, lr_ref, w2f_ref, w2b_ref, bf_ref, bb_ref, g_ref, o_ref,
                cumf_ref, cumb_ref, stf_ref, stb_ref):
    c = GLA_CHUNK
    n_chunks = q_ref.shape[1] // c
    lr = lr_ref[0].astype(BF16)
    rows = lambda n: pl.ds(pl.multiple_of(n * c, c), c)
    row = lax.broadcasted_iota(jnp.int32, (c, c), 0)
    col = lax.broadcasted_iota(jnp.int32, (c, c), 1)
    log_a = lambda w2_ref, b_ref: _log_sigmoid(_dot(lr, w2_ref[...]) + b_ref[...]) / GLA_TAU
    cumf_ref[...] = _chunk_cumsum(log_a(w2f_ref, bf_ref), reverse=False)
    cumb_ref[...] = _chunk_cumsum(log_a(w2b_ref, bb_ref), reverse=True)
    fwd = (cumf_ref, stf_ref, c - 1, row >= col)
    bwd = (cumb_ref, stb_ref, 0, row <= col)

    def step(n, state, direction):
        cum_ref, st_ref, last_row, _ = direction
        st_ref[n] = state.astype(BF16)
        cum = cum_ref[rows(n), :]
        last = cum[last_row:last_row + 1, :]
        kd = (k_ref[0, rows(n), :] * jnp.exp(last - cum)).astype(BF16)
        kv = lax.dot_general(v_ref[0, rows(n), :], kd, TN_DIMS, preferred_element_type=F32)
        return jnp.exp(last) * state + kv

    def scan(i, states):
        return step(i, states[0], fwd), step(n_chunks - 1 - i, states[1], bwd)

    zero = jnp.zeros((GLA_V, GLA_QK), F32)
    lax.fori_loop(0, n_chunks, scan, (zero, zero), unroll=4)

    def attend(n, direction):
        cum_ref, st_ref, _, keep = direction
        cum = cum_ref[rows(n), :]
        qg = (q_ref[0, rows(n), :] * jnp.exp(cum)).astype(BF16)
        kk = (k_ref[0, rows(n), :] * jnp.exp(-cum)).astype(BF16)
        sc = lax.dot_general(qg, kk, NT_DIMS, preferred_element_type=F32)
        return (_dot(jnp.where(keep, sc, 0.0).astype(BF16), v_ref[0, rows(n), :])
                + lax.dot_general(qg, st_ref[n], NT_DIMS, preferred_element_type=F32))

    def out(n, carry):
        o = attend(n, fwd) + attend(n, bwd)
        o_ref[0, rows(n), :] = (_rms(o, g_ref[...]) * _silu(gate_ref[0, rows(n), :])).astype(BF16)
        return carry

    lax.fori_loop(0, n_chunks, out, 0, unroll=4)


def _gla(q, k, v, gate, lr, w2_f, b_f, w2_b, b_b, g):
    b, s, _ = q.shape
    hk = GLA_HEADS * GLA_QK
    w2f = jnp.zeros((LANES, hk), BF16).at[:GLA_RANK].set(w2_f.astype(BF16))
    w2b = jnp.zeros((LANES, hk), BF16).at[GLA_RANK:2 * GLA_RANK].set(w2_b.astype(BF16))
    qk = pl.BlockSpec((1, s, GLA_QK), lambda bi, h: (bi, 0, h))
    vg = pl.BlockSpec((1, s, GLA_V), lambda bi, h: (bi, 0, h))
    w2 = pl.BlockSpec((LANES, GLA_QK), lambda bi, h: (0, h))
    bias = pl.BlockSpec((1, GLA_QK), lambda bi, h: (0, h))
    return pl.pallas_call(
        _gla_kernel,
        grid=(b, GLA_HEADS),
        in_specs=[qk, qk, vg, vg, pl.BlockSpec((1, s, LANES), lambda bi, h: (bi, 0, 0)),
                  w2, w2, bias, bias, _resident((1, GLA_V))],
        out_specs=vg,
        out_shape=jax.ShapeDtypeStruct((b, s, GLA_HEADS * GLA_V), BF16),
        scratch_shapes=[pltpu.VMEM((s, GLA_QK), F32), pltpu.VMEM((s, GLA_QK), F32),
                        pltpu.VMEM((s // GLA_CHUNK, GLA_V, GLA_QK), BF16),
                        pltpu.VMEM((s // GLA_CHUNK, GLA_V, GLA_QK), BF16)],
        compiler_params=_params(2),
        name="gla",
    )(q, k, v, gate, lr, w2f, w2b, b_f.reshape(1, -1), b_b.reshape(1, -1), g.reshape(1, -1))


def kernel(x, positions, ffn1_norm, ffn1_w_gate, ffn1_w_up, ffn1_w_down, ffn2_norm, ffn2_w_gate, ffn2_w_up, ffn2_w_down, ab_norm, ab_w_in, da_lq1, da_lk1, da_lq2, da_lk2, da_norm, ret_logit_f, ret_logit_b, ret_norm, ab_w_out, c_norm, c_w_in, gla_w2_f, gla_b_f, gla_w2_b, gla_b_b, gla_norm, c_w_out, final_norm):
    b, s, d = x.shape
    t = b * s
    tables = _rope_tables(positions)
    x = x.reshape(t, d)
    seq3 = lambda a: a.reshape(b, s, -1)
    tok2 = lambda a: a.reshape(t, -1)
    for layer in range(DEPTH):
        i = layer // 2
        x = _ffn(x, ffn1_norm[layer], ffn1_w_gate[layer], ffn1_w_up[layer], ffn1_w_down[layer],
                 final_norm, final_norm=False)
        if layer % 2 == 0:
            lam_init = 0.8 - 0.6 * math.exp(-0.3 * layer)
            qa, ka, va, qr, kr, vr, gr = _proj_ab(x, ab_norm[i], ab_w_in[i], tables, s)
            oa = _diff_attn(seq3(qa), seq3(ka), seq3(va), da_lq1[i], da_lk1[i], da_lq2[i], da_lk2[i],
                            da_norm[i], lam_init)
            orr = _retention(seq3(qr), seq3(kr), seq3(vr), seq3(gr), ret_logit_f[i], ret_logit_b[i],
                             ret_norm[i])
            x = _out_proj(x, [tok2(oa), tok2(orr)], ab_w_out[i])
        else:
            q, k, v, gate, lr = _proj_c(x, c_norm[i], c_w_in[i])
            o = _gla(seq3(q), seq3(k), seq3(v), seq3(gate), seq3(lr), gla_w2_f[i], gla_b_f[i],
                     gla_w2_b[i], gla_b_b[i], gla_norm[i])
            x = _out_proj(x, [tok2(o)], c_w_out[i])
        x = _ffn(x, ffn2_norm[layer], ffn2_w_gate[layer], ffn2_w_up[layer], ffn2_w_down[layer],
                 final_norm, final_norm=(layer == DEPTH - 1))
    return x.reshape(b, s, d)
```

```python
import functools
import math

import jax
import jax.numpy as jnp
import numpy as np
from jax import lax
from jax.experimental import pallas as pl
from jax.experimental.pallas import tpu as pltpu

D_MODEL = 1024
D_FF = 2816
DEPTH = 2
DA_HEADS = 4
DA_DIM = 64
DA_VDIM = 2 * DA_DIM
RET_HEADS = 4
RET_QK = 64
RET_V = 128
RET_CHUNK = 128
RET_THETA = 10000.0
GLA_HEADS = 4
GLA_QK = 128
GLA_V = 256
GLA_RANK = 16
GLA_TAU = 16.0
GLA_CHUNK = 64
ROPE_THETA = 500000.0
ROPE_FRAC = 4
EPS = 1e-6

LANES = 128
V7X_VMEM_LIMIT_BYTES = 56 * 1024 * 1024

F32 = jnp.float32
BF16 = jnp.bfloat16
NT_DIMS = (((1,), (1,)), ((), ()))
TN_DIMS = (((0,), (0,)), ((), ()))


def _params(n_grid_dims):
    return pltpu.CompilerParams(
        dimension_semantics=("arbitrary",) * n_grid_dims,
        vmem_limit_bytes=V7X_VMEM_LIMIT_BYTES,
    )


def _resident(shape, index_map=None):
    if index_map is None:
        index_map = lambda *_: (0,) * len(shape)
    return pl.BlockSpec(shape, index_map, pipeline_mode=pl.Buffered(1))


def _rms(x, g):
    return x * lax.rsqrt(jnp.mean(x * x, axis=-1, keepdims=True) + EPS) * g


def _silu(x):
    return x * jax.nn.sigmoid(x)


def _log_sigmoid(x):
    return jnp.minimum(x, 0.0) - jnp.log(1.0 + jnp.exp(-jnp.abs(x)))


def _dot(a, b):
    return jnp.dot(a, b, preferred_element_type=F32)


def _rope_table_kernel(pos_ref, inv_a_ref, inv_r_ref, ca_ref, sa_up_ref, sa_dn_ref,
                       cr_ref, sr_up_ref, sr_dn_ref):
    pos = pos_ref[...]
    lane = lax.broadcasted_iota(jnp.int32, (pos.shape[0], LANES), 1)
    rot = DA_DIM // ROPE_FRAC
    j = lane % DA_DIM
    ang = pos * inv_a_ref[...]
    c, s = jnp.cos(ang), jnp.sin(ang)
    ca_ref[...] = jnp.where(j < rot, c, 1.0)
    sa_up_ref[...] = jnp.where(j < rot // 2, -s, 0.0)
    sa_dn_ref[...] = jnp.where((j >= rot // 2) & (j < rot), s, 0.0)
    j = lane % RET_QK
    ang = pos * inv_r_ref[...]
    c, s = jnp.cos(ang), jnp.sin(ang)
    cr_ref[...] = c
    sr_up_ref[...] = jnp.where(j < RET_QK // 2, -s, 0.0)
    sr_dn_ref[...] = jnp.where(j >= RET_QK // 2, s, 0.0)


def _rope_tables(positions):
    seq = positions.shape[0]
    ts = 512
    rot = DA_DIM // ROPE_FRAC
    inv_a = 1.0 / (ROPE_THETA ** (jnp.arange(0, rot, 2, dtype=F32) / rot))
    inv_r = 1.0 / (RET_THETA ** (jnp.arange(0, RET_QK, 2, dtype=F32) / RET_QK))
    lane = np.arange(LANES)
    inv_a_lanes = inv_a[(lane % DA_DIM) % (rot // 2)][None, :]
    inv_r_lanes = inv_r[(lane % RET_QK) % (RET_QK // 2)][None, :]
    pos = positions.astype(F32).reshape(seq, 1)
    row = pl.BlockSpec((ts, LANES), lambda i: (i, 0))
    vec = pl.BlockSpec((1, LANES), lambda i: (0, 0))
    return pl.pallas_call(
        _rope_table_kernel,
        grid=(seq // ts,),
        in_specs=[pl.BlockSpec((ts, 1), lambda i: (i, 0)), vec, vec],
        out_specs=[row] * 6,
        out_shape=[jax.ShapeDtypeStruct((seq, LANES), F32)] * 6,
        compiler_params=_params(1),
        name="rope_tables",
    )(pos, inv_a_lanes, inv_r_lanes)


def _rope(x, c, s_up, s_dn, half):
    out = []
    for b in range(x.shape[1] // LANES):
        xb = x[:, b * LANES:(b + 1) * LANES]
        up = pltpu.roll(xb, LANES - half, 1)
        dn = pltpu.roll(xb, half, 1)
        out.append(xb * c + up * s_up + dn * s_dn)
    return jnp.concatenate(out, axis=1)


def _ffn_kernel(x_ref, g_ref, wg_ref, wu_ref, wd_ref, fg_ref, o_ref, *, final_norm):
    x = x_ref[...]
    xn = _rms(x, g_ref[...]).astype(BF16)
    gate = _dot(xn, wg_ref[...])
    up = _dot(xn, wu_ref[...])
    h = (_silu(gate) * up).astype(BF16)
    out = x + 0.5 * _dot(h, wd_ref[...])
    if final_norm:
        out = _rms(out, fg_ref[...])
    o_ref[...] = out


def _ffn(x, g, wg, wu, wd, fg, *, final_norm):
    t, d = x.shape
    tm = 512
    tok = pl.BlockSpec((tm, d), lambda i: (i, 0))
    return pl.pallas_call(
        functools.partial(_ffn_kernel, final_norm=final_norm),
        grid=(t // tm,),
        in_specs=[tok, _resident((1, d)), _resident((d, D_FF)), _resident((d, D_FF)),
                  _resident((D_FF, d)), _resident((1, d))],
        out_specs=tok,
        out_shape=jax.ShapeDtypeStruct((t, d), F32),
        compiler_params=_params(1),
        name="ffn",
    )(x, g.reshape(1, d), wg.astype(BF16), wu.astype(BF16), wd.astype(BF16), fg.reshape(1, d))


def _proj_ab_kernel(x_ref, g_ref, w_ref, ca_ref, sa_up_ref, sa_dn_ref, cr_ref, sr_up_ref, sr_dn_ref,
                    qa_ref, ka_ref, va_ref, qr_ref, kr_ref, vr_ref, gr_ref):
    xn = _rms(x_ref[...], g_ref[...]).astype(BF16)
    y = _dot(xn, w_ref[...])
    qa_w = DA_HEADS * 2 * DA_DIM
    va_w = DA_HEADS * DA_VDIM
    qr_w = RET_HEADS * RET_QK
    vr_w = RET_HEADS * RET_V
    o = 0
    qa, o = y[:, o:o + qa_w], o + qa_w
    ka, o = y[:, o:o + qa_w], o + qa_w
    va, o = y[:, o:o + va_w], o + va_w
    qr, o = y[:, o:o + qr_w], o + qr_w
    kr, o = y[:, o:o + qr_w], o + qr_w
    vr, o = y[:, o:o + vr_w], o + vr_w
    gr = y[:, o:o + vr_w]
    rope_a = functools.partial(_rope, c=ca_ref[...], s_up=sa_up_ref[...], s_dn=sa_dn_ref[...],
                               half=DA_DIM // ROPE_FRAC // 2)
    rope_r = functools.partial(_rope, c=cr_ref[...], s_up=sr_up_ref[...], s_dn=sr_dn_ref[...],
                               half=RET_QK // 2)
    qa_ref[...] = (rope_a(qa) * (DA_DIM ** -0.5 * math.log2(math.e))).astype(BF16)
    ka_ref[...] = rope_a(ka).astype(BF16)
    va_ref[...] = va.astype(BF16)
    qr_ref[...] = rope_r(qr)
    kr_ref[...] = rope_r(kr) * RET_QK ** -0.5
    vr_ref[...] = vr.astype(BF16)
    gr_ref[...] = gr


def _proj_ab(x, g, w, tables, seq):
    t, d = x.shape
    tm = 512
    n_in = w.shape[1]
    widths = (DA_HEADS * 2 * DA_DIM, DA_HEADS * 2 * DA_DIM, DA_HEADS * DA_VDIM,
              RET_HEADS * RET_QK, RET_HEADS * RET_QK, RET_HEADS * RET_V, RET_HEADS * RET_V)
    dtypes = (BF16, BF16, BF16, F32, F32, BF16, F32)
    tok = lambda wd: pl.BlockSpec((tm, wd), lambda i: (i, 0))
    tab = pl.BlockSpec((tm, LANES), lambda i: (i % (seq // tm), 0))
    return pl.pallas_call(
        _proj_ab_kernel,
        grid=(t // tm,),
        in_specs=[tok(d), _resident((1, d)), _resident((d, n_in))] + [tab] * 6,
        out_specs=[tok(wd) for wd in widths],
        out_shape=[jax.ShapeDtypeStruct((t, wd), dt) for wd, dt in zip(widths, dtypes)],
        compiler_params=_params(1),
        name="proj_ab",
    )(x, g.reshape(1, d), w.astype(BF16), *tables)


def _proj_c_kernel(x_ref, g_ref, w_ref, q_ref, k_ref, v_ref, gate_ref, lr_ref):
    xn = _rms(x_ref[...], g_ref[...]).astype(BF16)
    y = _dot(xn, w_ref[...])
    qk_w = GLA_HEADS * GLA_QK
    v_w = GLA_HEADS * GLA_V
    q_ref[...] = y[:, :qk_w] * GLA_QK ** -0.5
    k_ref[...] = y[:, qk_w:2 * qk_w]
    v_ref[...] = y[:, 2 * qk_w:2 * qk_w + v_w].astype(BF16)
    gate_ref[...] = y[:, 2 * qk_w + v_w:2 * qk_w + 2 * v_w]
    lr_ref[...] = y[:, 2 * qk_w + 2 * v_w:]


def _proj_c(x, g, w):
    t, d = x.shape
    tm = 512
    n_in = w.shape[1]
    w = jnp.pad(w, ((0, 0), (0, -n_in % LANES)))
    qk_w = GLA_HEADS * GLA_QK
    v_w = GLA_HEADS * GLA_V
    widths = (qk_w, qk_w, v_w, v_w, LANES)
    dtypes = (F32, F32, BF16, F32, F32)
    tok = lambda wd: pl.BlockSpec((tm, wd), lambda i: (i, 0))
    return pl.pallas_call(
        _proj_c_kernel,
        grid=(t // tm,),
        in_specs=[tok(d), _resident((1, d)), _resident((d, w.shape[1]))],
        out_specs=[tok(wd) for wd in widths],
        out_shape=[jax.ShapeDtypeStruct((t, wd), dt) for wd, dt in zip(widths, dtypes)],
        compiler_params=_params(1),
        name="proj_c",
    )(x, g.reshape(1, d), w.astype(BF16))


def _out_proj_kernel(*refs):
    x_ref, o_ref = refs[0], refs[-1]
    acc = x_ref[...]
    for a_ref, w_ref in zip(refs[1:-1:2], refs[2:-1:2]):
        acc = acc + _dot(a_ref[...], w_ref[...])
    o_ref[...] = acc


def _out_proj(x, acts, w):
    t, d = x.shape
    tm = 1024
    tok = lambda wd: pl.BlockSpec((tm, wd), lambda i: (i, 0))
    w = w.astype(BF16)
    in_specs, args, row = [tok(d)], [x], 0
    for a in acts:
        wd = a.shape[1]
        in_specs += [tok(wd), _resident((wd, d), functools.partial(lambda r, i: (r, 0), row // wd))]
        args += [a, w]
        row += wd
    return pl.pallas_call(
        _out_proj_kernel,
        grid=(t // tm,),
        in_specs=in_specs,
        out_specs=tok(d),
        out_shape=jax.ShapeDtypeStruct((t, d), F32),
        compiler_params=_params(1),
        name="out_proj",
    )(*args)


DA_KEY_TILE = 256


def _diff_attn_kernel(q_ref, k_ref, v_ref, lq1_ref, lk1_ref, lq2_ref, lk2_ref, g_ref, o_ref, *, lam_init):
    q = q_ref[0]
    lane = lax.broadcasted_iota(jnp.int32, q.shape, 1)
    zero = jnp.zeros_like(q)
    q_comp = (jnp.where(lane < DA_DIM, q, zero), jnp.where(lane >= DA_DIM, q, zero))
    scores = [lax.dot_general(qc, k_ref[0], NT_DIMS, preferred_element_type=F32) for qc in q_comp]
    row_max = [jnp.max(s, axis=-1, keepdims=True) for s in scores]
    tq, seq = scores[0].shape
    ones = jnp.ones((DA_KEY_TILE, LANES), BF16)
    pv = [jnp.zeros((tq, DA_VDIM + LANES), F32) for _ in scores]
    for j in range(seq // DA_KEY_TILE):
        cols = slice(j * DA_KEY_TILE, (j + 1) * DA_KEY_TILE)
        v_ones = jnp.concatenate([v_ref[0, cols, :], ones], axis=1)
        for c, s in enumerate(scores):
            pj = jnp.exp2(s[:, cols] - row_max[c]).astype(BF16)
            pv[c] = pv[c] + _dot(pj, v_ones)
    lam = (jnp.exp(jnp.sum(lq1_ref[...] * lk1_ref[...], axis=-1, keepdims=True))
           - jnp.exp(jnp.sum(lq2_ref[...] * lk2_ref[...], axis=-1, keepdims=True)) + lam_init)
    soft = [x[:, :DA_VDIM] / x[:, DA_VDIM:] for x in pv]
    o = soft[0] - lam * soft[1]
    o_ref[0] = (_rms(o, g_ref[...]) * (1.0 - lam_init)).astype(BF16)


def _diff_attn(q, k, v, lq1, lk1, lq2, lk2, g, lam_init):
    b, s, _ = q.shape
    tq = 512
    hw = 2 * DA_DIM
    vec = lambda n: _resident((1, n))
    return pl.pallas_call(
        functools.partial(_diff_attn_kernel, lam_init=lam_init),
        grid=(b, DA_HEADS, s // tq),
        in_specs=[pl.BlockSpec((1, tq, hw), lambda bi, h, qi: (bi, qi, h)),
                  pl.BlockSpec((1, s, hw), lambda bi, h, qi: (bi, 0, h)),
                  pl.BlockSpec((1, s, DA_VDIM), lambda bi, h, qi: (bi, 0, h)),
                  vec(DA_DIM), vec(DA_DIM), vec(DA_DIM), vec(DA_DIM), vec(DA_VDIM)],
        out_specs=pl.BlockSpec((1, tq, DA_VDIM), lambda bi, h, qi: (bi, qi, h)),
        out_shape=jax.ShapeDtypeStruct((b, s, DA_HEADS * DA_VDIM), BF16),
        compiler_params=_params(3),
        name="diff_attn",
    )(q, k, v, lq1.reshape(1, -1), lk1.reshape(1, -1), lq2.reshape(1, -1), lk2.reshape(1, -1),
      g.reshape(1, -1))


RET_PAIR = LANES // RET_QK


def _retention_kernel(q_ref, k_ref, v_ref, gate_ref, lf_ref, lb_ref, g_ref, o_ref, st_ref):
    c = RET_CHUNK
    n_chunks = q_ref.shape[1] // c
    pw = RET_PAIR * RET_V
    row = lax.broadcasted_iota(jnp.int32, (c, c), 0).astype(F32)
    col = lax.broadcasted_iota(jnp.int32, (c, c), 1).astype(F32)
    lane_head = lax.broadcasted_iota(jnp.int32, (c, LANES), 1) // RET_QK
    rows = lambda n: pl.ds(pl.multiple_of(n * c, c), c)
    heads = range(RET_PAIR)
    lg_f = [_log_sigmoid(lf_ref[j]) for j in heads]
    lg_b = [_log_sigmoid(lb_ref[j]) for j in heads]

    def per_lane(lg):
        out = lg[0]
        for j in heads[1:]:
            out = jnp.where(lane_head[:1] == j, lg[j], out)
        return out

    per_col = lambda lg: jnp.concatenate([jnp.broadcast_to(x, (1, RET_V)) for x in lg], axis=1)
    diff = row - col
    decay = [jnp.where(diff >= 0, jnp.exp(lg_f[j] * jnp.maximum(diff, 0.0)), 0.0)
             + jnp.where(diff <= 0, jnp.exp(lg_b[j] * jnp.maximum(-diff, 0.0)), 0.0) for j in heads]
    lgl_f, lgl_b = per_lane(lg_f), per_lane(lg_b)
    q_dec_f = jnp.exp(lgl_f * (row + 1.0))
    k_dec_f = jnp.exp(lgl_f * (c - 1.0 - row))
    q_dec_b = jnp.exp(lgl_b * (c - row))
    k_dec_b = jnp.exp(lgl_b * row)
    chunk_dec_f = jnp.exp(per_col(lg_f) * c)
    chunk_dec_b = jnp.exp(per_col(lg_b) * c)
    own_block = (lax.broadcasted_iota(jnp.int32, (LANES, pw), 0) // RET_QK
                 == lax.broadcasted_iota(jnp.int32, (LANES, pw), 1) // RET_V)

    def kv(n, k_dec):
        kd = (k_ref[0, rows(n), :] * k_dec).astype(BF16)
        out = lax.dot_general(kd, v_ref[0, rows(n), :], TN_DIMS, preferred_element_type=F32)
        return jnp.where(own_block, out, 0.0)

    def scan(i, states):
        sf, sb = states
        nf, nb = i, n_chunks - 1 - i
        st_ref[nf, :LANES, :] = sf.astype(BF16)
        st_ref[nb, LANES:, :] = sb.astype(BF16)
        return chunk_dec_f * sf + kv(nf, k_dec_f), chunk_dec_b * sb + kv(nb, k_dec_b)

    zero = jnp.zeros((LANES, pw), F32)
    lax.fori_loop(0, n_chunks, scan, (zero, zero), unroll=4)

    def out(n, carry):
        qc = q_ref[0, rows(n), :]
        kc = k_ref[0, rows(n), :].astype(BF16)
        vc = v_ref[0, rows(n), :]
        q_heads = jnp.concatenate([jnp.where(lane_head == j, qc, 0.0) for j in heads], axis=0)
        sc = lax.dot_general(q_heads.astype(BF16), kc, NT_DIMS, preferred_element_type=F32)
        inter = _dot(jnp.concatenate([qc * q_dec_f, qc * q_dec_b], axis=1).astype(BF16), st_ref[n])
        o = []
        for j in heads:
            a = (sc[j * c:(j + 1) * c] * decay[j]).astype(BF16)
            oj = _dot(a, vc[:, j * RET_V:(j + 1) * RET_V]) + inter[:, j * RET_V:(j + 1) * RET_V]
            o.append(_rms(oj, g_ref[...]))
        o = jnp.concatenate(o, axis=1) * _silu(gate_ref[0, rows(n), :])
        o_ref[0, rows(n), :] = o.astype(BF16)
        return carry

    lax.fori_loop(0, n_chunks, out, 0, unroll=4)


def _retention(q, k, v, gate, logit_f, logit_b, g):
    b, s, _ = q.shape
    n_chunks = s // RET_CHUNK
    pw = RET_PAIR * RET_V
    lanes = lambda x: jnp.broadcast_to(x.astype(F32)[:, None, None], (RET_HEADS, 1, LANES))
    qk = pl.BlockSpec((1, s, LANES), lambda bi, p: (bi, 0, p))
    vg = pl.BlockSpec((1, s, pw), lambda bi, p: (bi, 0, p))
    logit = pl.BlockSpec((RET_PAIR, 1, LANES), lambda bi, p: (p, 0, 0))
    return pl.pallas_call(
        _retention_kernel,
        grid=(b, RET_HEADS // RET_PAIR),
        in_specs=[qk, qk, vg, vg, logit, logit, _resident((1, RET_V))],
        out_specs=vg,
        out_shape=jax.ShapeDtypeStruct((b, s, RET_HEADS * RET_V), BF16),
        scratch_shapes=[pltpu.VMEM((n_chunks, 2 * LANES, pw), BF16)],
        compiler_params=_params(2),
        name="retention",
    )(q, k, v, gate, lanes(logit_f), lanes(logit_b), g.reshape(1, -1))


GLA_GROUP = 4


def _chunk_cumsum(x, reverse):
    n = x.shape[0]
    pos = lax.broadcasted_iota(jnp.int32, x.shape, 0) % GLA_CHUNK
    step = 1
    while step < GLA_CHUNK:
        if reverse:
            x = x + jnp.where(pos < GLA_CHUNK - step, pltpu.roll(x, n - step, 0), 0.0)
        else:
            x = x + jnp.where(pos >= step, pltpu.roll(x, step, 0), 0.0)
        step *= 2
    return x


def _gla_kernel(q_ref, k_ref, v_ref, gate_ref, lr_ref, w2f_ref, w2b_ref, bf_ref, bb_ref, g_ref, o_ref,
                cumf_ref, cumb_ref, stf_ref, stb_ref):
    c = GLA_CHUNK
    n_chunks = q_ref.shape[1] // c
    lr = lr_ref[0].astype(BF16)
    rows = lambda n: pl.ds(pl.multiple_of(n * c, c), c)
    log_a = lambda w2_ref, b_ref: _log_sigmoid(_dot(lr, w2_ref[...]) + b_ref[...]) / GLA_TAU
    cumf_ref[...] = _chunk_cumsum(log_a(w2f_ref, bf_ref), reverse=False)
    cumb_ref[...] = _chunk_cumsum(log_a(w2b_ref, bb_ref), reverse=True)
    gsz = GLA_GROUP * c
    row = lax.broadcasted_iota(jnp.int32, (gsz, gsz), 0)
    col = lax.broadcasted_iota(jnp.int32, (gsz, gsz), 1)
    same_chunk = (row // c) == (col // c)
    fwd = (cumf_ref, stf_ref, c - 1, same_chunk & (row >= col))
    bwd = (cumb_ref, stb_ref, 0, same_chunk & (row <= col))

    def step(n, state, direction):
        cum_ref, st_ref, last_row, _ = direction
        st_ref[n] = state.astype(BF16)
        cum = cum_ref[rows(n), :]
        last = cum[last_row:last_row + 1, :]
        kd = (k_ref[0, rows(n), :] * jnp.exp(last - cum)).astype(BF16)
        kv = lax.dot_general(v_ref[0, rows(n), :], kd, TN_DIMS, preferred_element_type=F32)
        return jnp.exp(last) * state + kv

    def scan(i, states):
        return step(i, states[0], fwd), step(n_chunks - 1 - i, states[1], bwd)

    zero = jnp.zeros((GLA_V, GLA_QK), F32)
    lax.fori_loop(0, n_chunks, scan, (zero, zero), unroll=4)

    def attend(g, direction):
        cum_ref, st_ref, _, keep = direction
        grows = pl.ds(pl.multiple_of(g * gsz, gsz), gsz)
        cum = cum_ref[grows, :]
        qg = (q_ref[0, grows, :] * jnp.exp(cum)).astype(BF16)
        kk = (k_ref[0, grows, :] * jnp.exp(-cum)).astype(BF16)
        sc = lax.dot_general(qg, kk, NT_DIMS, preferred_element_type=F32)
        intra = _dot(jnp.where(keep, sc, 0.0).astype(BF16), v_ref[0, grows, :])
        inter = [lax.dot_general(qg[i * c:(i + 1) * c], st_ref[g * GLA_GROUP + i], NT_DIMS,
                                 preferred_element_type=F32) for i in range(GLA_GROUP)]
        return intra + jnp.concatenate(inter, axis=0)

    def out(g, carry):
        grows = pl.ds(pl.multiple_of(g * gsz, gsz), gsz)
        o = attend(g, fwd) + attend(g, bwd)
        o_ref[0, grows, :] = (_rms(o, g_ref[...]) * _silu(gate_ref[0, grows, :])).astype(BF16)
        return carry

    lax.fori_loop(0, n_chunks // GLA_GROUP, out, 0, unroll=2)


def _gla(q, k, v, gate, lr, w2_f, b_f, w2_b, b_b, g):
    b, s, _ = q.shape
    hk = GLA_HEADS * GLA_QK
    w2f = jnp.zeros((LANES, hk), BF16).at[:GLA_RANK].set(w2_f.astype(BF16))
    w2b = jnp.zeros((LANES, hk), BF16).at[GLA_RANK:2 * GLA_RANK].set(w2_b.astype(BF16))
    qk = pl.BlockSpec((1, s, GLA_QK), lambda bi, h: (bi, 0, h))
    vg = pl.BlockSpec((1, s, GLA_V), lambda bi, h: (bi, 0, h))
    w2 = pl.BlockSpec((LANES, GLA_QK), lambda bi, h: (0, h))
    bias = pl.BlockSpec((1, GLA_QK), lambda bi, h: (0, h))
    return pl.pallas_call(
        _gla_kernel,
        grid=(b, GLA_HEADS),
        in_specs=[qk, qk, vg, vg, pl.BlockSpec((1, s, LANES), lambda bi, h: (bi, 0, 0)),
                  w2, w2, bias, bias, _resident((1, GLA_V))],
        out_specs=vg,
        out_shape=jax.ShapeDtypeStruct((b, s, GLA_HEADS * GLA_V), BF16),
        scratch_shapes=[pltpu.VMEM((s, GLA_QK), F32), pltpu.VMEM((s, GLA_QK), F32),
                        pltpu.VMEM((s // GLA_CHUNK, GLA_V, GLA_QK), BF16),
                        pltpu.VMEM((s // GLA_CHUNK, GLA_V, GLA_QK), BF16)],
        compiler_params=_params(2),
        name="gla",
    )(q, k, v, gate, lr, w2f, w2b, b_f.reshape(1, -1), b_b.reshape(1, -1), g.reshape(1, -1))


def kernel(x, positions, ffn1_norm, ffn1_w_gate, ffn1_w_up, ffn1_w_down, ffn2_norm, ffn2_w_gate, ffn2_w_up, ffn2_w_down, ab_norm, ab_w_in, da_lq1, da_lk1, da_lq2, da_lk2, da_norm, ret_logit_f, ret_logit_b, ret_norm, ab_w_out, c_norm, c_w_in, gla_w2_f, gla_b_f, gla_w2_b, gla_b_b, gla_norm, c_w_out, final_norm):
    b, s, d = x.shape
    t = b * s
    tables = _rope_tables(positions)
    x = x.reshape(t, d)
    seq3 = lambda a: a.reshape(b, s, -1)
    tok2 = lambda a: a.reshape(t, -1)
    for layer in range(DEPTH):
        i = layer // 2
        x = _ffn(x, ffn1_norm[layer], ffn1_w_gate[layer], ffn1_w_up[layer], ffn1_w_down[layer],
                 final_norm, final_norm=False)
        if layer % 2 == 0:
            lam_init = 0.8 - 0.6 * math.exp(-0.3 * layer)
            qa, ka, va, qr, kr, vr, gr = _proj_ab(x, ab_norm[i], ab_w_in[i], tables, s)
            oa = _diff_attn(seq3(qa), seq3(ka), seq3(va), da_lq1[i], da_lk1[i], da_lq2[i], da_lk2[i],
                            da_norm[i], lam_init)
            orr = _retention(seq3(qr), seq3(kr), seq3(vr), seq3(gr), ret_logit_f[i], ret_logit_b[i],
                             ret_norm[i])
            x = _out_proj(x, [tok2(oa), tok2(orr)], ab_w_out[i])
        else:
            q, k, v, gate, lr = _proj_c(x, c_norm[i], c_w_in[i])
            o = _gla(seq3(q), seq3(k), seq3(v), seq3(gate), seq3(lr), gla_w2_f[i], gla_b_f[i],
                     gla_w2_b[i], gla_b_b[i], gla_norm[i])
            x = _out_proj(x, [tok2(o)], c_w_out[i])
        x = _ffn(x, ffn2_norm[layer], ffn2_w_gate[layer], ffn2_w_up[layer], ffn2_w_down[layer],
                 final_norm, final_norm=(layer == DEPTH - 1))
    return x.reshape(b, s, d)
```

```python
import functools
import math

import jax
import jax.numpy as jnp
import numpy as np
from jax import lax
from jax.experimental import pallas as pl
from jax.experimental.pallas import tpu as pltpu

D_MODEL = 1024
D_FF = 2816
DEPTH = 2
DA_HEADS = 4
DA_DIM = 64
DA_VDIM = 2 * DA_DIM
RET_HEADS = 4
RET_QK = 64
RET_V = 128
RET_CHUNK = 128
RET_THETA = 10000.0
GLA_HEADS = 4
GLA_QK = 128
GLA_V = 256
GLA_RANK = 16
GLA_TAU = 16.0
GLA_CHUNK = 64
ROPE_THETA = 500000.0
ROPE_FRAC = 4
EPS = 1e-6

LANES = 128
V7X_VMEM_LIMIT_BYTES = 56 * 1024 * 1024

F32 = jnp.float32
BF16 = jnp.bfloat16
NT_DIMS = (((1,), (1,)), ((), ()))
TN_DIMS = (((0,), (0,)), ((), ()))


def _params(n_grid_dims):
    return pltpu.CompilerParams(
        dimension_semantics=("arbitrary",) * n_grid_dims,
        vmem_limit_bytes=V7X_VMEM_LIMIT_BYTES,
    )


def _resident(shape, index_map=None):
    if index_map is None:
        index_map = lambda *_: (0,) * len(shape)
    return pl.BlockSpec(shape, index_map, pipeline_mode=pl.Buffered(1))


def _rms(x, g):
    return x * lax.rsqrt(jnp.mean(x * x, axis=-1, keepdims=True) + EPS) * g


def _silu(x):
    return x * jax.nn.sigmoid(x)


def _log_sigmoid(x):
    return jnp.minimum(x, 0.0) - jnp.log(1.0 + jnp.exp(-jnp.abs(x)))


def _dot(a, b):
    return jnp.dot(a, b, preferred_element_type=F32)


def _rope_table_kernel(pos_ref, inv_a_ref, inv_r_ref, ca_ref, sa_up_ref, sa_dn_ref,
                       cr_ref, sr_up_ref, sr_dn_ref):
    pos = pos_ref[...]
    lane = lax.broadcasted_iota(jnp.int32, (pos.shape[0], LANES), 1)
    rot = DA_DIM // ROPE_FRAC
    j = lane % DA_DIM
    ang = pos * inv_a_ref[...]
    c, s = jnp.cos(ang), jnp.sin(ang)
    ca_ref[...] = jnp.where(j < rot, c, 1.0)
    sa_up_ref[...] = jnp.where(j < rot // 2, -s, 0.0)
    sa_dn_ref[...] = jnp.where((j >= rot // 2) & (j < rot), s, 0.0)
    j = lane % RET_QK
    ang = pos * inv_r_ref[...]
    c, s = jnp.cos(ang), jnp.sin(ang)
    cr_ref[...] = c
    sr_up_ref[...] = jnp.where(j < RET_QK // 2, -s, 0.0)
    sr_dn_ref[...] = jnp.where(j >= RET_QK // 2, s, 0.0)


def _rope_tables(positions):
    seq = positions.shape[0]
    ts = 512
    rot = DA_DIM // ROPE_FRAC
    inv_a = 1.0 / (ROPE_THETA ** (jnp.arange(0, rot, 2, dtype=F32) / rot))
    inv_r = 1.0 / (RET_THETA ** (jnp.arange(0, RET_QK, 2, dtype=F32) / RET_QK))
    lane = np.arange(LANES)
    inv_a_lanes = inv_a[(lane % DA_DIM) % (rot // 2)][None, :]
    inv_r_lanes = inv_r[(lane % RET_QK) % (RET_QK // 2)][None, :]
    pos = positions.astype(F32).reshape(seq, 1)
    row = pl.BlockSpec((ts, LANES), lambda i: (i, 0))
    vec = pl.BlockSpec((1, LANES), lambda i: (0, 0))
    return pl.pallas_call(
        _rope_table_kernel,
        grid=(seq // ts,),
        in_specs=[pl.BlockSpec((ts, 1), lambda i: (i, 0)), vec, vec],
        out_specs=[row] * 6,
        out_shape=[jax.ShapeDtypeStruct((seq, LANES), F32)] * 6,
        compiler_params=_params(1),
        name="rope_tables",
    )(pos, inv_a_lanes, inv_r_lanes)


def _rope(x, c, s_up, s_dn, half):
    out = []
    for b in range(x.shape[1] // LANES):
        xb = x[:, b * LANES:(b + 1) * LANES]
        up = pltpu.roll(xb, LANES - half, 1)
        dn = pltpu.roll(xb, half, 1)
        out.append(xb * c + up * s_up + dn * s_dn)
    return jnp.concatenate(out, axis=1)


def _ffn_kernel(*refs, n_mixed, final_norm):
    x_ref, mixed = refs[0], refs[1:1 + 2 * n_mixed]
    g_ref, wg_ref, wu_ref, wd_ref, fg_ref, o_ref = refs[1 + 2 * n_mixed:]
    x = x_ref[...]
    for a_ref, w_ref in zip(mixed[::2], mixed[1::2]):
        x = x + _dot(a_ref[...], w_ref[...])
    xn = _rms(x, g_ref[...]).astype(BF16)
    gate = _dot(xn, wg_ref[...])
    up = _dot(xn, wu_ref[...])
    h = (_silu(gate) * up).astype(BF16)
    out = x + 0.5 * _dot(h, wd_ref[...])
    if final_norm:
        out = _rms(out, fg_ref[...])
    o_ref[...] = out


def _ffn(x, g, wg, wu, wd, fg, *, final_norm, mixed=(), w_out=None):
    t, d = x.shape
    tm = 512
    tok = lambda wd_: pl.BlockSpec((tm, wd_), lambda i: (i, 0))
    in_specs, args, row = [tok(d)], [x], 0
    for a in mixed:
        wa = a.shape[1]
        in_specs += [tok(wa), _resident((wa, d), functools.partial(lambda r, i: (r, 0), row // wa))]
        args += [a, w_out.astype(BF16)]
        row += wa
    in_specs += [_resident((1, d)), _resident((d, D_FF)), _resident((d, D_FF)), _resident((D_FF, d)),
                 _resident((1, d))]
    args += [g.reshape(1, d), wg.astype(BF16), wu.astype(BF16), wd.astype(BF16), fg.reshape(1, d)]
    return pl.pallas_call(
        functools.partial(_ffn_kernel, n_mixed=len(mixed), final_norm=final_norm),
        grid=(t // tm,),
        in_specs=in_specs,
        out_specs=tok(d),
        out_shape=jax.ShapeDtypeStruct((t, d), F32),
        compiler_params=_params(1),
        name="ffn",
    )(*args)


def _proj_ab_kernel(x_ref, g_ref, w_ref, ca_ref, sa_up_ref, sa_dn_ref, cr_ref, sr_up_ref, sr_dn_ref,
                    qa_ref, ka_ref, va_ref, qr_ref, kr_ref, vr_ref, gr_ref):
    xn = _rms(x_ref[...], g_ref[...]).astype(BF16)
    y = _dot(xn, w_ref[...])
    qa_w = DA_HEADS * 2 * DA_DIM
    va_w = DA_HEADS * DA_VDIM
    qr_w = RET_HEADS * RET_QK
    vr_w = RET_HEADS * RET_V
    o = 0
    qa, o = y[:, o:o + qa_w], o + qa_w
    ka, o = y[:, o:o + qa_w], o + qa_w
    va, o = y[:, o:o + va_w], o + va_w
    qr, o = y[:, o:o + qr_w], o + qr_w
    kr, o = y[:, o:o + qr_w], o + qr_w
    vr, o = y[:, o:o + vr_w], o + vr_w
    gr = y[:, o:o + vr_w]
    rope_a = functools.partial(_rope, c=ca_ref[...], s_up=sa_up_ref[...], s_dn=sa_dn_ref[...],
                               half=DA_DIM // ROPE_FRAC // 2)
    rope_r = functools.partial(_rope, c=cr_ref[...], s_up=sr_up_ref[...], s_dn=sr_dn_ref[...],
                               half=RET_QK // 2)
    qa_ref[...] = (rope_a(qa) * (DA_DIM ** -0.5 * math.log2(math.e))).astype(BF16)
    ka_ref[...] = rope_a(ka).astype(BF16)
    va_ref[...] = va.astype(BF16)
    qr_ref[...] = rope_r(qr)
    kr_ref[...] = rope_r(kr) * RET_QK ** -0.5
    vr_ref[...] = vr.astype(BF16)
    gr_ref[...] = gr


def _proj_ab(x, g, w, tables, seq):
    t, d = x.shape
    tm = 512
    n_in = w.shape[1]
    widths = (DA_HEADS * 2 * DA_DIM, DA_HEADS * 2 * DA_DIM, DA_HEADS * DA_VDIM,
              RET_HEADS * RET_QK, RET_HEADS * RET_QK, RET_HEADS * RET_V, RET_HEADS * RET_V)
    dtypes = (BF16, BF16, BF16, F32, F32, BF16, F32)
    tok = lambda wd: pl.BlockSpec((tm, wd), lambda i: (i, 0))
    tab = pl.BlockSpec((tm, LANES), lambda i: (i % (seq // tm), 0))
    return pl.pallas_call(
        _proj_ab_kernel,
        grid=(t // tm,),
        in_specs=[tok(d), _resident((1, d)), _resident((d, n_in))] + [tab] * 6,
        out_specs=[tok(wd) for wd in widths],
        out_shape=[jax.ShapeDtypeStruct((t, wd), dt) for wd, dt in zip(widths, dtypes)],
        compiler_params=_params(1),
        name="proj_ab",
    )(x, g.reshape(1, d), w.astype(BF16), *tables)


def _proj_c_kernel(x_ref, g_ref, w_ref, q_ref, k_ref, v_ref, gate_ref, lr_ref):
    xn = _rms(x_ref[...], g_ref[...]).astype(BF16)
    y = _dot(xn, w_ref[...])
    qk_w = GLA_HEADS * GLA_QK
    v_w = GLA_HEADS * GLA_V
    q_ref[...] = y[:, :qk_w] * GLA_QK ** -0.5
    k_ref[...] = y[:, qk_w:2 * qk_w]
    v_ref[...] = y[:, 2 * qk_w:2 * qk_w + v_w].astype(BF16)
    gate_ref[...] = y[:, 2 * qk_w + v_w:2 * qk_w + 2 * v_w]
    lr_ref[...] = y[:, 2 * qk_w + 2 * v_w:]


def _proj_c(x, g, w):
    t, d = x.shape
    tm = 512
    n_in = w.shape[1]
    w = jnp.pad(w, ((0, 0), (0, -n_in % LANES)))
    qk_w = GLA_HEADS * GLA_QK
    v_w = GLA_HEADS * GLA_V
    widths = (qk_w, qk_w, v_w, v_w, LANES)
    dtypes = (F32, F32, BF16, F32, F32)
    tok = lambda wd: pl.BlockSpec((tm, wd), lambda i: (i, 0))
    return pl.pallas_call(
        _proj_c_kernel,
        grid=(t // tm,),
        in_specs=[tok(d), _resident((1, d)), _resident((d, w.shape[1]))],
        out_specs=[tok(wd) for wd in widths],
        out_shape=[jax.ShapeDtypeStruct((t, wd), dt) for wd, dt in zip(widths, dtypes)],
        compiler_params=_params(1),
        name="proj_c",
    )(x, g.reshape(1, d), w.astype(BF16))


DA_KEY_TILE = 256


def _diff_attn_kernel(q_ref, k_ref, v_ref, lq1_ref, lk1_ref, lq2_ref, lk2_ref, g_ref, o_ref, *, lam_init):
    q = q_ref[0]
    lane = lax.broadcasted_iota(jnp.int32, q.shape, 1)
    zero = jnp.zeros_like(q)
    q_comp = (jnp.where(lane < DA_DIM, q, zero), jnp.where(lane >= DA_DIM, q, zero))
    scores = [lax.dot_general(qc, k_ref[0], NT_DIMS, preferred_element_type=F32) for qc in q_comp]
    row_max = [jnp.max(s, axis=-1, keepdims=True) for s in scores]
    tq, seq = scores[0].shape
    ones = jnp.ones((DA_KEY_TILE, LANES), BF16)
    pv = [jnp.zeros((tq, DA_VDIM + LANES), F32) for _ in scores]
    for j in range(seq // DA_KEY_TILE):
        cols = slice(j * DA_KEY_TILE, (j + 1) * DA_KEY_TILE)
        v_ones = jnp.concatenate([v_ref[0, cols, :], ones], axis=1)
        for c, s in enumerate(scores):
            pj = jnp.exp2(s[:, cols] - row_max[c]).astype(BF16)
            pv[c] = pv[c] + _dot(pj, v_ones)
    lam = (jnp.exp(jnp.sum(lq1_ref[...] * lk1_ref[...], axis=-1, keepdims=True))
           - jnp.exp(jnp.sum(lq2_ref[...] * lk2_ref[...], axis=-1, keepdims=True)) + lam_init)
    soft = [x[:, :DA_VDIM] / x[:, DA_VDIM:] for x in pv]
    o = soft[0] - lam * soft[1]
    o_ref[0] = (_rms(o, g_ref[...]) * (1.0 - lam_init)).astype(BF16)


def _diff_attn(q, k, v, lq1, lk1, lq2, lk2, g, lam_init):
    b, s, _ = q.shape
    tq = 512
    hw = 2 * DA_DIM
    vec = lambda n: _resident((1, n))
    return pl.pallas_call(
        functools.partial(_diff_attn_kernel, lam_init=lam_init),
        grid=(b, DA_HEADS, s // tq),
        in_specs=[pl.BlockSpec((1, tq, hw), lambda bi, h, qi: (bi, qi, h)),
                  pl.BlockSpec((1, s, hw), lambda bi, h, qi: (bi, 0, h)),
                  pl.BlockSpec((1, s, DA_VDIM), lambda bi, h, qi: (bi, 0, h)),
                  vec(DA_DIM), vec(DA_DIM), vec(DA_DIM), vec(DA_DIM), vec(DA_VDIM)],
        out_specs=pl.BlockSpec((1, tq, DA_VDIM), lambda bi, h, qi: (bi, qi, h)),
        out_shape=jax.ShapeDtypeStruct((b, s, DA_HEADS * DA_VDIM), BF16),
        compiler_params=_params(3),
        name="diff_attn",
    )(q, k, v, lq1.reshape(1, -1), lk1.reshape(1, -1), lq2.reshape(1, -1), lk2.reshape(1, -1),
      g.reshape(1, -1))


RET_PAIR = LANES // RET_QK


def _retention_kernel(q_ref, k_ref, v_ref, gate_ref, lf_ref, lb_ref, g_ref, o_ref, st_ref):
    c = RET_CHUNK
    n_chunks = q_ref.shape[1] // c
    pw = RET_PAIR * RET_V
    row = lax.broadcasted_iota(jnp.int32, (c, c), 0).astype(F32)
    col = lax.broadcasted_iota(jnp.int32, (c, c), 1).astype(F32)
    lane_head = lax.broadcasted_iota(jnp.int32, (c, LANES), 1) // RET_QK
    rows = lambda n: pl.ds(pl.multiple_of(n * c, c), c)
    heads = range(RET_PAIR)
    lg_f = [_log_sigmoid(lf_ref[j]) for j in heads]
    lg_b = [_log_sigmoid(lb_ref[j]) for j in heads]

    def per_lane(lg):
        out = lg[0]
        for j in heads[1:]:
            out = jnp.where(lane_head[:1] == j, lg[j], out)
        return out

    per_col = lambda lg: jnp.concatenate([jnp.broadcast_to(x, (1, RET_V)) for x in lg], axis=1)
    diff = row - col
    decay = [jnp.where(diff >= 0, jnp.exp(lg_f[j] * jnp.maximum(diff, 0.0)), 0.0)
             + jnp.where(diff <= 0, jnp.exp(lg_b[j] * jnp.maximum(-diff, 0.0)), 0.0) for j in heads]
    lgl_f, lgl_b = per_lane(lg_f), per_lane(lg_b)
    q_dec_f = jnp.exp(lgl_f * (row + 1.0))
    k_dec_f = jnp.exp(lgl_f * (c - 1.0 - row))
    q_dec_b = jnp.exp(lgl_b * (c - row))
    k_dec_b = jnp.exp(lgl_b * row)
    chunk_dec_f = jnp.exp(per_col(lg_f) * c)
    chunk_dec_b = jnp.exp(per_col(lg_b) * c)
    own_block = (lax.broadcasted_iota(jnp.int32, (LANES, pw), 0) // RET_QK
                 == lax.broadcasted_iota(jnp.int32, (LANES, pw), 1) // RET_V)

    def kv(n, k_dec):
        kd = (k_ref[0, rows(n), :] * k_dec).astype(BF16)
        out = lax.dot_general(kd, v_ref[0, rows(n), :], TN_DIMS, preferred_element_type=F32)
        return jnp.where(own_block, out, 0.0)

    def scan(i, states):
        sf, sb = states
        nf, nb = i, n_chunks - 1 - i
        st_ref[nf, :LANES, :] = sf.astype(BF16)
        st_ref[nb, LANES:, :] = sb.astype(BF16)
        return chunk_dec_f * sf + kv(nf, k_dec_f), chunk_dec_b * sb + kv(nb, k_dec_b)

    zero = jnp.zeros((LANES, pw), F32)
    lax.fori_loop(0, n_chunks, scan, (zero, zero), unroll=4)

    def out(n, carry):
        qc = q_ref[0, rows(n), :]
        kc = k_ref[0, rows(n), :].astype(BF16)
        vc = v_ref[0, rows(n), :]
        q_heads = jnp.concatenate([jnp.where(lane_head == j, qc, 0.0) for j in heads], axis=0)
        sc = lax.dot_general(q_heads.astype(BF16), kc, NT_DIMS, preferred_element_type=F32)
        inter = _dot(jnp.concatenate([qc * q_dec_f, qc * q_dec_b], axis=1).astype(BF16), st_ref[n])
        o = []
        for j in heads:
            a = (sc[j * c:(j + 1) * c] * decay[j]).astype(BF16)
            oj = _dot(a, vc[:, j * RET_V:(j + 1) * RET_V]) + inter[:, j * RET_V:(j + 1) * RET_V]
            o.append(_rms(oj, g_ref[...]))
        o = jnp.concatenate(o, axis=1) * _silu(gate_ref[0, rows(n), :])
        o_ref[0, rows(n), :] = o.astype(BF16)
        return carry

    lax.fori_loop(0, n_chunks, out, 0, unroll=4)


def _retention(q, k, v, gate, logit_f, logit_b, g):
    b, s, _ = q.shape
    n_chunks = s // RET_CHUNK
    pw = RET_PAIR * RET_V
    lanes = lambda x: jnp.broadcast_to(x.astype(F32)[:, None, None], (RET_HEADS, 1, LANES))
    qk = pl.BlockSpec((1, s, LANES), lambda bi, p: (bi, 0, p))
    vg = pl.BlockSpec((1, s, pw), lambda bi, p: (bi, 0, p))
    logit = pl.BlockSpec((RET_PAIR, 1, LANES), lambda bi, p: (p, 0, 0))
    return pl.pallas_call(
        _retention_kernel,
        grid=(b, RET_HEADS // RET_PAIR),
        in_specs=[qk, qk, vg, vg, logit, logit, _resident((1, RET_V))],
        out_specs=vg,
        out_shape=jax.ShapeDtypeStruct((b, s, RET_HEADS * RET_V), BF16),
        scratch_shapes=[pltpu.VMEM((n_chunks, 2 * LANES, pw), BF16)],
        compiler_params=_params(2),
        name="retention",
    )(q, k, v, gate, lanes(logit_f), lanes(logit_b), g.reshape(1, -1))


GLA_GROUP = 4


def _chunk_cumsum(x, reverse):
    n = x.shape[0]
    pos = lax.broadcasted_iota(jnp.int32, x.shape, 0) % GLA_CHUNK
    step = 1
    while step < GLA_CHUNK:
        if reverse:
            x = x + jnp.where(pos < GLA_CHUNK - step, pltpu.roll(x, n - step, 0), 0.0)
        else:
            x = x + jnp.where(pos >= step, pltpu.roll(x, step, 0), 0.0)
        step *= 2
    return x


def _gla_kernel(q_ref, k_ref, v_ref, gate_ref, lr_ref, w2f_ref, w2b_ref, bf_ref, bb_ref, g_ref, o_ref,
                cumf_ref, cumb_ref, stf_ref, stb_ref):
    c = GLA_CHUNK
    n_chunks = q_ref.shape[1] // c
    lr = lr_ref[0].astype(BF16)
    rows = lambda n: pl.ds(pl.multiple_of(n * c, c), c)
    log_a = lambda w2_ref, b_ref: _log_sigmoid(_dot(lr, w2_ref[...]) + b_ref[...]) / GLA_TAU
    cumf_ref[...] = _chunk_cumsum(log_a(w2f_ref, bf_ref), reverse=False)
    cumb_ref[...] = _chunk_cumsum(log_a(w2b_ref, bb_ref), reverse=True)
    gsz = GLA_GROUP * c
    row = lax.broadcasted_iota(jnp.int32, (gsz, gsz), 0)
    col = lax.broadcasted_iota(jnp.int32, (gsz, gsz), 1)
    same_chunk = (row // c) == (col // c)
    fwd = (cumf_ref, stf_ref, c - 1, same_chunk & (row >= col))
    bwd = (cumb_ref, stb_ref, 0, same_chunk & (row <= col))

    def step(n, state, direction):
        cum_ref, st_ref, last_row, _ = direction
        st_ref[n] = state.astype(BF16)
        cum = cum_ref[rows(n), :]
        last = cum[last_row:last_row + 1, :]
        kd = (k_ref[0, rows(n), :] * jnp.exp(last - cum)).astype(BF16)
        kv = lax.dot_general(v_ref[0, rows(n), :], kd, TN_DIMS, preferred_element_type=F32)
        return jnp.exp(last) * state + kv

    def scan(i, states):
        return step(i, states[0], fwd), step(n_chunks - 1 - i, states[1], bwd)

    zero = jnp.zeros((GLA_V, GLA_QK), F32)
    lax.fori_loop(0, n_chunks, scan, (zero, zero), unroll=4)

    def attend(g, direction):
        cum_ref, st_ref, _, keep = direction
        grows = pl.ds(pl.multiple_of(g * gsz, gsz), gsz)
        cum = cum_ref[grows, :]
        qg = (q_ref[0, grows, :] * jnp.exp(cum)).astype(BF16)
        kk = (k_ref[0, grows, :] * jnp.exp(-cum)).astype(BF16)
        sc = lax.dot_general(qg, kk, NT_DIMS, preferred_element_type=F32)
        intra = _dot(jnp.where(keep, sc, 0.0).astype(BF16), v_ref[0, grows, :])
        inter = [lax.dot_general(qg[i * c:(i + 1) * c], st_ref[g * GLA_GROUP + i], NT_DIMS,
                                 preferred_element_type=F32) for i in range(GLA_GROUP)]
        return intra + jnp.concatenate(inter, axis=0)

    def out(g, carry):
        grows = pl.ds(pl.multiple_of(g * gsz, gsz), gsz)
        o = attend(g, fwd) + attend(g, bwd)
        o_ref[0, grows, :] = (_rms(o, g_ref[...]) * _silu(gate_ref[0, grows, :])).astype(BF16)
        return carry

    lax.fori_loop(0, n_chunks // GLA_GROUP, out, 0, unroll=2)


def _gla(q, k, v, gate, lr, w2_f, b_f, w2_b, b_b, g):
    b, s, _ = q.shape
    hk = GLA_HEADS * GLA_QK
    w2f = jnp.zeros((LANES, hk), BF16).at[:GLA_RANK].set(w2_f.astype(BF16))
    w2b = jnp.zeros((LANES, hk), BF16).at[GLA_RANK:2 * GLA_RANK].set(w2_b.astype(BF16))
    qk = pl.BlockSpec((1, s, GLA_QK), lambda bi, h: (bi, 0, h))
    vg = pl.BlockSpec((1, s, GLA_V), lambda bi, h: (bi, 0, h))
    w2 = pl.BlockSpec((LANES, GLA_QK), lambda bi, h: (0, h))
    bias = pl.BlockSpec((1, GLA_QK), lambda bi, h: (0, h))
    return pl.pallas_call(
        _gla_kernel,
        grid=(b, GLA_HEADS),
        in_specs=[qk, qk, vg, vg, pl.BlockSpec((1, s, LANES), lambda bi, h: (bi, 0, 0)),
                  w2, w2, bias, bias, _resident((1, GLA_V))],
        out_specs=vg,
        out_shape=jax.ShapeDtypeStruct((b, s, GLA_HEADS * GLA_V), BF16),
        scratch_shapes=[pltpu.VMEM((s, GLA_QK), F32), pltpu.VMEM((s, GLA_QK), F32),
                        pltpu.VMEM((s // GLA_CHUNK, GLA_V, GLA_QK), BF16),
                        pltpu.VMEM((s // GLA_CHUNK, GLA_V, GLA_QK), BF16)],
        compiler_params=_params(2),
        name="gla",
    )(q, k, v, gate, lr, w2f, w2b, b_f.reshape(1, -1), b_b.reshape(1, -1), g.reshape(1, -1))


def kernel(x, positions, ffn1_norm, ffn1_w_gate, ffn1_w_up, ffn1_w_down, ffn2_norm, ffn2_w_gate, ffn2_w_up, ffn2_w_down, ab_norm, ab_w_in, da_lq1, da_lk1, da_lq2, da_lk2, da_norm, ret_logit_f, ret_logit_b, ret_norm, ab_w_out, c_norm, c_w_in, gla_w2_f, gla_b_f, gla_w2_b, gla_b_b, gla_norm, c_w_out, final_norm):
    b, s, d = x.shape
    t = b * s
    tables = _rope_tables(positions)
    x = x.reshape(t, d)
    seq3 = lambda a: a.reshape(b, s, -1)
    tok2 = lambda a: a.reshape(t, -1)
    for layer in range(DEPTH):
        i = layer // 2
        x = _ffn(x, ffn1_norm[layer], ffn1_w_gate[layer], ffn1_w_up[layer], ffn1_w_down[layer],
                 final_norm, final_norm=False)
        if layer % 2 == 0:
            lam_init = 0.8 - 0.6 * math.exp(-0.3 * layer)
            qa, ka, va, qr, kr, vr, gr = _proj_ab(x, ab_norm[i], ab_w_in[i], tables, s)
            oa = _diff_attn(seq3(qa), seq3(ka), seq3(va), da_lq1[i], da_lk1[i], da_lq2[i], da_lk2[i],
                            da_norm[i], lam_init)
            orr = _retention(seq3(qr), seq3(kr), seq3(vr), seq3(gr), ret_logit_f[i], ret_logit_b[i],
                             ret_norm[i])
            mixed, w_out = [tok2(oa), tok2(orr)], ab_w_out[i]
        else:
            q, k, v, gate, lr = _proj_c(x, c_norm[i], c_w_in[i])
            o = _gla(seq3(q), seq3(k), seq3(v), seq3(gate), seq3(lr), gla_w2_f[i], gla_b_f[i],
                     gla_w2_b[i], gla_b_b[i], gla_norm[i])
            mixed, w_out = [tok2(o)], c_w_out[i]
        x = _ffn(x, ffn2_norm[layer], ffn2_w_gate[layer], ffn2_w_up[layer], ffn2_w_down[layer],
                 final_norm, final_norm=(layer == DEPTH - 1), mixed=mixed, w_out=w_out)
    return x.reshape(b, s, d)
```

```python
import functools
import math

import jax
import jax.numpy as jnp
import numpy as np
from jax import lax
from jax.experimental import pallas as pl
from jax.experimental.pallas import tpu as pltpu

D_MODEL = 1024
D_FF = 2816
DEPTH = 2
DA_HEADS = 4
DA_DIM = 64
DA_VDIM = 2 * DA_DIM
RET_HEADS = 4
RET_QK = 64
RET_V = 128
RET_CHUNK = 128
RET_THETA = 10000.0
GLA_HEADS = 4
GLA_QK = 128
GLA_V = 256
GLA_RANK = 16
GLA_TAU = 16.0
GLA_CHUNK = 64
ROPE_THETA = 500000.0
ROPE_FRAC = 4
EPS = 1e-6

LANES = 128
V7X_VMEM_LIMIT_BYTES = 56 * 1024 * 1024

F32 = jnp.float32
BF16 = jnp.bfloat16
NT_DIMS = (((1,), (1,)), ((), ()))
TN_DIMS = (((0,), (0,)), ((), ()))


def _params(n_grid_dims):
    return pltpu.CompilerParams(
        dimension_semantics=("arbitrary",) * n_grid_dims,
        vmem_limit_bytes=V7X_VMEM_LIMIT_BYTES,
    )


def _resident(shape, index_map=None):
    if index_map is None:
        index_map = lambda *_: (0,) * len(shape)
    return pl.BlockSpec(shape, index_map, pipeline_mode=pl.Buffered(1))


def _rms(x, g):
    return x * lax.rsqrt(jnp.mean(x * x, axis=-1, keepdims=True) + EPS) * g


def _silu(x):
    return x * jax.nn.sigmoid(x)


def _log_sigmoid(x):
    return jnp.minimum(x, 0.0) - jnp.log(1.0 + jnp.exp(-jnp.abs(x)))


def _dot(a, b):
    return jnp.dot(a, b, preferred_element_type=F32)


def _rope_table_kernel(pos_ref, inv_a_ref, inv_r_ref, ca_ref, sa_up_ref, sa_dn_ref,
                       cr_ref, sr_up_ref, sr_dn_ref):
    pos = pos_ref[...]
    lane = lax.broadcasted_iota(jnp.int32, (pos.shape[0], LANES), 1)
    rot = DA_DIM // ROPE_FRAC
    j = lane % DA_DIM
    ang = pos * inv_a_ref[...]
    c, s = jnp.cos(ang), jnp.sin(ang)
    ca_ref[...] = jnp.where(j < rot, c, 1.0)
    sa_up_ref[...] = jnp.where(j < rot // 2, -s, 0.0)
    sa_dn_ref[...] = jnp.where((j >= rot // 2) & (j < rot), s, 0.0)
    j = lane % RET_QK
    ang = pos * inv_r_ref[...]
    c, s = jnp.cos(ang), jnp.sin(ang)
    cr_ref[...] = c
    sr_up_ref[...] = jnp.where(j < RET_QK // 2, -s, 0.0)
    sr_dn_ref[...] = jnp.where(j >= RET_QK // 2, s, 0.0)


def _rope_tables(positions):
    seq = positions.shape[0]
    ts = 512
    rot = DA_DIM // ROPE_FRAC
    inv_a = 1.0 / (ROPE_THETA ** (jnp.arange(0, rot, 2, dtype=F32) / rot))
    inv_r = 1.0 / (RET_THETA ** (jnp.arange(0, RET_QK, 2, dtype=F32) / RET_QK))
    lane = np.arange(LANES)
    inv_a_lanes = inv_a[(lane % DA_DIM) % (rot // 2)][None, :]
    inv_r_lanes = inv_r[(lane % RET_QK) % (RET_QK // 2)][None, :]
    pos = positions.astype(F32).reshape(seq, 1)
    row = pl.BlockSpec((ts, LANES), lambda i: (i, 0))
    vec = pl.BlockSpec((1, LANES), lambda i: (0, 0))
    return pl.pallas_call(
        _rope_table_kernel,
        grid=(seq // ts,),
        in_specs=[pl.BlockSpec((ts, 1), lambda i: (i, 0)), vec, vec],
        out_specs=[row] * 6,
        out_shape=[jax.ShapeDtypeStruct((seq, LANES), F32)] * 6,
        compiler_params=_params(1),
        name="rope_tables",
    )(pos, inv_a_lanes, inv_r_lanes)


def _rope(x, c, s_up, s_dn, half):
    out = []
    for b in range(x.shape[1] // LANES):
        xb = x[:, b * LANES:(b + 1) * LANES]
        up = pltpu.roll(xb, LANES - half, 1)
        dn = pltpu.roll(xb, half, 1)
        out.append(xb * c + up * s_up + dn * s_dn)
    return jnp.concatenate(out, axis=1)


def _ffn_kernel(*refs, n_mixed, final_norm):
    x_ref, mixed = refs[0], refs[1:1 + 2 * n_mixed]
    g_ref, wg_ref, wu_ref, wd_ref, fg_ref, o_ref = refs[1 + 2 * n_mixed:]
    x = x_ref[...]
    for a_ref, w_ref in zip(mixed[::2], mixed[1::2]):
        x = x + _dot(a_ref[...], w_ref[...])
    xn = _rms(x, g_ref[...]).astype(BF16)
    gate = _dot(xn, wg_ref[...])
    up = _dot(xn, wu_ref[...])
    h = (_silu(gate) * up).astype(BF16)
    out = x + 0.5 * _dot(h, wd_ref[...])
    if final_norm:
        out = _rms(out, fg_ref[...])
    o_ref[...] = out


def _ffn(x, layer, g, wg, wu, wd, fg, *, final_norm, mixed=(), w_out=None):
    t, d = x.shape
    tm = 512
    tok = lambda wd_: pl.BlockSpec((tm, wd_), lambda i: (i, 0))
    of_layer = lambda rows, cols: _resident((None, rows, cols), lambda i: (layer, 0, 0))
    in_specs, args, row = [tok(d)], [x], 0
    for a in mixed:
        wa = a.shape[1]
        in_specs += [tok(wa), _resident((wa, d), functools.partial(lambda r, i: (r, 0), row // wa))]
        args += [a, w_out]
        row += wa
    in_specs += [of_layer(1, d), of_layer(d, D_FF), of_layer(d, D_FF), of_layer(D_FF, d), _resident((1, d))]
    args += [g.reshape(-1, 1, d), wg, wu, wd, fg.reshape(1, d)]
    return pl.pallas_call(
        functools.partial(_ffn_kernel, n_mixed=len(mixed), final_norm=final_norm),
        grid=(t // tm,),
        in_specs=in_specs,
        out_specs=tok(d),
        out_shape=jax.ShapeDtypeStruct((t, d), F32),
        compiler_params=_params(1),
        name="ffn",
    )(*args)


def _proj_ab_kernel(x_ref, g_ref, w_ref, ca_ref, sa_up_ref, sa_dn_ref, cr_ref, sr_up_ref, sr_dn_ref,
                    qa_ref, ka_ref, va_ref, qr_ref, kr_ref, vr_ref, gr_ref):
    xn = _rms(x_ref[...], g_ref[...]).astype(BF16)
    y = _dot(xn, w_ref[...])
    qa_w = DA_HEADS * 2 * DA_DIM
    va_w = DA_HEADS * DA_VDIM
    qr_w = RET_HEADS * RET_QK
    vr_w = RET_HEADS * RET_V
    o = 0
    qa, o = y[:, o:o + qa_w], o + qa_w
    ka, o = y[:, o:o + qa_w], o + qa_w
    va, o = y[:, o:o + va_w], o + va_w
    qr, o = y[:, o:o + qr_w], o + qr_w
    kr, o = y[:, o:o + qr_w], o + qr_w
    vr, o = y[:, o:o + vr_w], o + vr_w
    gr = y[:, o:o + vr_w]
    rope_a = functools.partial(_rope, c=ca_ref[...], s_up=sa_up_ref[...], s_dn=sa_dn_ref[...],
                               half=DA_DIM // ROPE_FRAC // 2)
    rope_r = functools.partial(_rope, c=cr_ref[...], s_up=sr_up_ref[...], s_dn=sr_dn_ref[...],
                               half=RET_QK // 2)
    qa_ref[...] = (rope_a(qa) * (DA_DIM ** -0.5 * math.log2(math.e))).astype(BF16)
    ka_ref[...] = rope_a(ka).astype(BF16)
    va_ref[...] = va.astype(BF16)
    qr_ref[...] = rope_r(qr)
    kr_ref[...] = rope_r(kr) * RET_QK ** -0.5
    vr_ref[...] = vr.astype(BF16)
    gr_ref[...] = gr


def _proj_ab(x, g, w, tables, seq):
    t, d = x.shape
    tm = 512
    n_in = w.shape[1]
    widths = (DA_HEADS * 2 * DA_DIM, DA_HEADS * 2 * DA_DIM, DA_HEADS * DA_VDIM,
              RET_HEADS * RET_QK, RET_HEADS * RET_QK, RET_HEADS * RET_V, RET_HEADS * RET_V)
    dtypes = (BF16, BF16, BF16, F32, F32, BF16, F32)
    tok = lambda wd: pl.BlockSpec((tm, wd), lambda i: (i, 0))
    tab = pl.BlockSpec((tm, LANES), lambda i: (i % (seq // tm), 0))
    return pl.pallas_call(
        _proj_ab_kernel,
        grid=(t // tm,),
        in_specs=[tok(d), _resident((1, d)), _resident((d, n_in))] + [tab] * 6,
        out_specs=[tok(wd) for wd in widths],
        out_shape=[jax.ShapeDtypeStruct((t, wd), dt) for wd, dt in zip(widths, dtypes)],
        compiler_params=_params(1),
        name="proj_ab",
    )(x, g.reshape(1, d), w.astype(BF16), *tables)


def _proj_c_kernel(x_ref, g_ref, w_ref, q_ref, k_ref, v_ref, gate_ref, lr_ref):
    xn = _rms(x_ref[...], g_ref[...]).astype(BF16)
    y = _dot(xn, w_ref[...])
    qk_w = GLA_HEADS * GLA_QK
    v_w = GLA_HEADS * GLA_V
    q_ref[...] = y[:, :qk_w] * GLA_QK ** -0.5
    k_ref[...] = y[:, qk_w:2 * qk_w]
    v_ref[...] = y[:, 2 * qk_w:2 * qk_w + v_w].astype(BF16)
    gate_ref[...] = y[:, 2 * qk_w + v_w:2 * qk_w + 2 * v_w]
    lr_ref[...] = y[:, 2 * qk_w + 2 * v_w:].astype(BF16)


def _proj_c(x, g, w):
    t, d = x.shape
    tm = 512
    n_in = w.shape[1]
    w = jnp.pad(w, ((0, 0), (0, -n_in % LANES)))
    qk_w = GLA_HEADS * GLA_QK
    v_w = GLA_HEADS * GLA_V
    widths = (qk_w, qk_w, v_w, v_w, LANES)
    dtypes = (F32, F32, BF16, F32, BF16)
    tok = lambda wd: pl.BlockSpec((tm, wd), lambda i: (i, 0))
    return pl.pallas_call(
        _proj_c_kernel,
        grid=(t // tm,),
        in_specs=[tok(d), _resident((1, d)), _resident((d, w.shape[1]))],
        out_specs=[tok(wd) for wd in widths],
        out_shape=[jax.ShapeDtypeStruct((t, wd), dt) for wd, dt in zip(widths, dtypes)],
        compiler_params=_params(1),
        name="proj_c",
    )(x, g.reshape(1, d), w.astype(BF16))


DA_KEY_TILE = 256


def _diff_attn_kernel(q_ref, k_ref, v_ref, lq1_ref, lk1_ref, lq2_ref, lk2_ref, g_ref, o_ref, *, lam_init):
    q = q_ref[0]
    lane = lax.broadcasted_iota(jnp.int32, q.shape, 1)
    zero = jnp.zeros_like(q)
    q_comp = (jnp.where(lane < DA_DIM, q, zero), jnp.where(lane >= DA_DIM, q, zero))
    scores = [lax.dot_general(qc, k_ref[0], NT_DIMS, preferred_element_type=F32) for qc in q_comp]
    row_max = [jnp.max(s, axis=-1, keepdims=True) for s in scores]
    tq, seq = scores[0].shape
    ones = jnp.ones((DA_KEY_TILE, LANES), BF16)
    pv = [jnp.zeros((tq, DA_VDIM + LANES), F32) for _ in scores]
    for j in range(seq // DA_KEY_TILE):
        cols = slice(j * DA_KEY_TILE, (j + 1) * DA_KEY_TILE)
        v_ones = jnp.concatenate([v_ref[0, cols, :], ones], axis=1)
        for c, s in enumerate(scores):
            pj = jnp.exp2(s[:, cols] - row_max[c]).astype(BF16)
            pv[c] = pv[c] + _dot(pj, v_ones)
    lam = (jnp.exp(jnp.sum(lq1_ref[...] * lk1_ref[...], axis=-1, keepdims=True))
           - jnp.exp(jnp.sum(lq2_ref[...] * lk2_ref[...], axis=-1, keepdims=True)) + lam_init)
    soft = [x[:, :DA_VDIM] / x[:, DA_VDIM:] for x in pv]
    o = soft[0] - lam * soft[1]
    o_ref[0] = (_rms(o, g_ref[...]) * (1.0 - lam_init)).astype(BF16)


def _diff_attn(q, k, v, lq1, lk1, lq2, lk2, g, lam_init):
    b, s, _ = q.shape
    tq = 1024
    hw = 2 * DA_DIM
    vec = lambda n: _resident((1, n))
    return pl.pallas_call(
        functools.partial(_diff_attn_kernel, lam_init=lam_init),
        grid=(b, DA_HEADS, s // tq),
        in_specs=[pl.BlockSpec((1, tq, hw), lambda bi, h, qi: (bi, qi, h)),
                  pl.BlockSpec((1, s, hw), lambda bi, h, qi: (bi, 0, h)),
                  pl.BlockSpec((1, s, DA_VDIM), lambda bi, h, qi: (bi, 0, h)),
                  vec(DA_DIM), vec(DA_DIM), vec(DA_DIM), vec(DA_DIM), vec(DA_VDIM)],
        out_specs=pl.BlockSpec((1, tq, DA_VDIM), lambda bi, h, qi: (bi, qi, h)),
        out_shape=jax.ShapeDtypeStruct((b, s, DA_HEADS * DA_VDIM), BF16),
        compiler_params=_params(3),
        name="diff_attn",
    )(q, k, v, lq1.reshape(1, -1), lk1.reshape(1, -1), lq2.reshape(1, -1), lk2.reshape(1, -1),
      g.reshape(1, -1))


RET_PAIR = LANES // RET_QK


def _retention_kernel(q_ref, k_ref, v_ref, gate_ref, lf_ref, lb_ref, g_ref, o_ref, st_ref):
    c = RET_CHUNK
    n_chunks = q_ref.shape[1] // c
    pw = RET_PAIR * RET_V
    row = lax.broadcasted_iota(jnp.int32, (c, c), 0).astype(F32)
    col = lax.broadcasted_iota(jnp.int32, (c, c), 1).astype(F32)
    lane_head = lax.broadcasted_iota(jnp.int32, (c, LANES), 1) // RET_QK
    rows = lambda n: pl.ds(pl.multiple_of(n * c, c), c)
    heads = range(RET_PAIR)
    lg_f = [_log_sigmoid(lf_ref[j]) for j in heads]
    lg_b = [_log_sigmoid(lb_ref[j]) for j in heads]

    def per_lane(lg):
        out = lg[0]
        for j in heads[1:]:
            out = jnp.where(lane_head[:1] == j, lg[j], out)
        return out

    per_col = lambda lg: jnp.concatenate([jnp.broadcast_to(x, (1, RET_V)) for x in lg], axis=1)
    diff = row - col
    decay = [jnp.where(diff >= 0, jnp.exp(lg_f[j] * jnp.maximum(diff, 0.0)), 0.0)
             + jnp.where(diff <= 0, jnp.exp(lg_b[j] * jnp.maximum(-diff, 0.0)), 0.0) for j in heads]
    lgl_f, lgl_b = per_lane(lg_f), per_lane(lg_b)
    q_dec_f = jnp.exp(lgl_f * (row + 1.0))
    k_dec_f = jnp.exp(lgl_f * (c - 1.0 - row))
    q_dec_b = jnp.exp(lgl_b * (c - row))
    k_dec_b = jnp.exp(lgl_b * row)
    chunk_dec_f = jnp.exp(per_col(lg_f) * c)
    chunk_dec_b = jnp.exp(per_col(lg_b) * c)
    own_block = (lax.broadcasted_iota(jnp.int32, (LANES, pw), 0) // RET_QK
                 == lax.broadcasted_iota(jnp.int32, (LANES, pw), 1) // RET_V)

    def kv(n, k_dec):
        kd = (k_ref[0, rows(n), :] * k_dec).astype(BF16)
        out = lax.dot_general(kd, v_ref[0, rows(n), :], TN_DIMS, preferred_element_type=F32)
        return jnp.where(own_block, out, 0.0)

    def scan(i, states):
        sf, sb = states
        nf, nb = i, n_chunks - 1 - i
        st_ref[nf, :LANES, :] = sf.astype(BF16)
        st_ref[nb, LANES:, :] = sb.astype(BF16)
        return chunk_dec_f * sf + kv(nf, k_dec_f), chunk_dec_b * sb + kv(nb, k_dec_b)

    zero = jnp.zeros((LANES, pw), F32)
    lax.fori_loop(0, n_chunks, scan, (zero, zero), unroll=8)

    def out(n, carry):
        qc = q_ref[0, rows(n), :]
        kc = k_ref[0, rows(n), :].astype(BF16)
        vc = v_ref[0, rows(n), :]
        q_heads = jnp.concatenate([jnp.where(lane_head == j, qc, 0.0) for j in heads], axis=0)
        sc = lax.dot_general(q_heads.astype(BF16), kc, NT_DIMS, preferred_element_type=F32)
        inter = _dot(jnp.concatenate([qc * q_dec_f, qc * q_dec_b], axis=1).astype(BF16), st_ref[n])
        o = []
        for j in heads:
            a = (sc[j * c:(j + 1) * c] * decay[j]).astype(BF16)
            oj = _dot(a, vc[:, j * RET_V:(j + 1) * RET_V]) + inter[:, j * RET_V:(j + 1) * RET_V]
            o.append(_rms(oj, g_ref[...]))
        o = jnp.concatenate(o, axis=1) * _silu(gate_ref[0, rows(n), :])
        o_ref[0, rows(n), :] = o.astype(BF16)
        return carry

    lax.fori_loop(0, n_chunks, out, 0, unroll=8)


def _retention(q, k, v, gate, logit_f, logit_b, g):
    b, s, _ = q.shape
    n_chunks = s // RET_CHUNK
    pw = RET_PAIR * RET_V
    lanes = lambda x: jnp.broadcast_to(x.astype(F32)[:, None, None], (RET_HEADS, 1, LANES))
    qk = pl.BlockSpec((1, s, LANES), lambda bi, p: (bi, 0, p))
    vg = pl.BlockSpec((1, s, pw), lambda bi, p: (bi, 0, p))
    logit = pl.BlockSpec((RET_PAIR, 1, LANES), lambda bi, p: (p, 0, 0))
    return pl.pallas_call(
        _retention_kernel,
        grid=(b, RET_HEADS // RET_PAIR),
        in_specs=[qk, qk, vg, vg, logit, logit, _resident((1, RET_V))],
        out_specs=vg,
        out_shape=jax.ShapeDtypeStruct((b, s, RET_HEADS * RET_V), BF16),
        scratch_shapes=[pltpu.VMEM((n_chunks, 2 * LANES, pw), BF16)],
        compiler_params=_params(2),
        name="retention",
    )(q, k, v, gate, lanes(logit_f), lanes(logit_b), g.reshape(1, -1))


GLA_GROUP = 4


def _chunk_cumsum(x, reverse):
    n = x.shape[0]
    pos = lax.broadcasted_iota(jnp.int32, x.shape, 0) % GLA_CHUNK
    step = 1
    while step < GLA_CHUNK:
        if reverse:
            x = x + jnp.where(pos < GLA_CHUNK - step, pltpu.roll(x, n - step, 0), 0.0)
        else:
            x = x + jnp.where(pos >= step, pltpu.roll(x, step, 0), 0.0)
        step *= 2
    return x


def _gla_kernel(q_ref, k_ref, v_ref, gate_ref, lr_ref, w2f_ref, w2b_ref, bf_ref, bb_ref, g_ref, o_ref,
                cumf_ref, cumb_ref, stf_ref, stb_ref):
    c = GLA_CHUNK
    n_chunks = q_ref.shape[1] // c
    lr = lr_ref[0]
    rows = lambda n: pl.ds(pl.multiple_of(n * c, c), c)
    log_a = lambda w2_ref, b_ref: _log_sigmoid(_dot(lr, w2_ref[...]) + b_ref[...]) / GLA_TAU
    cumf_ref[...] = _chunk_cumsum(log_a(w2f_ref, bf_ref), reverse=False)
    cumb_ref[...] = _chunk_cumsum(log_a(w2b_ref, bb_ref), reverse=True)
    gsz = GLA_GROUP * c
    row = lax.broadcasted_iota(jnp.int32, (gsz, gsz), 0)
    col = lax.broadcasted_iota(jnp.int32, (gsz, gsz), 1)
    same_chunk = (row // c) == (col // c)
    fwd = (cumf_ref, stf_ref, c - 1, same_chunk & (row >= col))
    bwd = (cumb_ref, stb_ref, 0, same_chunk & (row <= col))

    def step(n, state, direction):
        cum_ref, st_ref, last_row, _ = direction
        st_ref[n] = state.astype(BF16)
        cum = cum_ref[rows(n), :]
        last = cum[last_row:last_row + 1, :]
        kd = (k_ref[0, rows(n), :] * jnp.exp(last - cum)).astype(BF16)
        kv = lax.dot_general(v_ref[0, rows(n), :], kd, TN_DIMS, preferred_element_type=F32)
        return jnp.exp(last) * state + kv

    def scan(i, states):
        return step(i, states[0], fwd), step(n_chunks - 1 - i, states[1], bwd)

    zero = jnp.zeros((GLA_V, GLA_QK), F32)
    lax.fori_loop(0, n_chunks, scan, (zero, zero), unroll=8)

    def attend(g, direction):
        cum_ref, st_ref, _, keep = direction
        grows = pl.ds(pl.multiple_of(g * gsz, gsz), gsz)
        cum = cum_ref[grows, :]
        qg = (q_ref[0, grows, :] * jnp.exp(cum)).astype(BF16)
        kk = (k_ref[0, grows, :] * jnp.exp(-cum)).astype(BF16)
        sc = lax.dot_general(qg, kk, NT_DIMS, preferred_element_type=F32)
        intra = _dot(jnp.where(keep, sc, 0.0).astype(BF16), v_ref[0, grows, :])
        inter = [lax.dot_general(qg[i * c:(i + 1) * c], st_ref[g * GLA_GROUP + i], NT_DIMS,
                                 preferred_element_type=F32) for i in range(GLA_GROUP)]
        return intra + jnp.concatenate(inter, axis=0)

    def out(g, carry):
        grows = pl.ds(pl.multiple_of(g * gsz, gsz), gsz)
        o = attend(g, fwd) + attend(g, bwd)
        o_ref[0, grows, :] = (_rms(o, g_ref[...]) * _silu(gate_ref[0, grows, :])).astype(BF16)
        return carry

    lax.fori_loop(0, n_chunks // GLA_GROUP, out, 0, unroll=4)


def _gla(q, k, v, gate, lr, w2_f, b_f, w2_b, b_b, g):
    b, s, _ = q.shape
    hk = GLA_HEADS * GLA_QK
    w2f = jnp.zeros((LANES, hk), BF16).at[:GLA_RANK].set(w2_f.astype(BF16))
    w2b = jnp.zeros((LANES, hk), BF16).at[GLA_RANK:2 * GLA_RANK].set(w2_b.astype(BF16))
    qk = pl.BlockSpec((1, s, GLA_QK), lambda bi, h: (bi, 0, h))
    vg = pl.BlockSpec((1, s, GLA_V), lambda bi, h: (bi, 0, h))
    w2 = pl.BlockSpec((LANES, GLA_QK), lambda bi, h: (0, h))
    bias = pl.BlockSpec((1, GLA_QK), lambda bi, h: (0, h))
    return pl.pallas_call(
        _gla_kernel,
        grid=(b, GLA_HEADS),
        in_specs=[qk, qk, vg, vg, pl.BlockSpec((1, s, LANES), lambda bi, h: (bi, 0, 0)),
                  w2, w2, bias, bias, _resident((1, GLA_V))],
        out_specs=vg,
        out_shape=jax.ShapeDtypeStruct((b, s, GLA_HEADS * GLA_V), BF16),
        scratch_shapes=[pltpu.VMEM((s, GLA_QK), F32), pltpu.VMEM((s, GLA_QK), F32),
                        pltpu.VMEM((s // GLA_CHUNK, GLA_V, GLA_QK), BF16),
                        pltpu.VMEM((s // GLA_CHUNK, GLA_V, GLA_QK), BF16)],
        compiler_params=_params(2),
        name="gla",
    )(q, k, v, gate, lr, w2f, w2b, b_f.reshape(1, -1), b_b.reshape(1, -1), g.reshape(1, -1))


def kernel(x, positions, ffn1_norm, ffn1_w_gate, ffn1_w_up, ffn1_w_down, ffn2_norm, ffn2_w_gate, ffn2_w_up, ffn2_w_down, ab_norm, ab_w_in, da_lq1, da_lk1, da_lq2, da_lk2, da_norm, ret_logit_f, ret_logit_b, ret_norm, ab_w_out, c_norm, c_w_in, gla_w2_f, gla_b_f, gla_w2_b, gla_b_b, gla_norm, c_w_out, final_norm):
    b, s, d = x.shape
    t = b * s
    tables = _rope_tables(positions)
    x = x.reshape(t, d)
    seq3 = lambda a: a.reshape(b, s, -1)
    tok2 = lambda a: a.reshape(t, -1)
    ffn1 = (ffn1_norm, ffn1_w_gate.astype(BF16), ffn1_w_up.astype(BF16), ffn1_w_down.astype(BF16))
    ffn2 = (ffn2_norm, ffn2_w_gate.astype(BF16), ffn2_w_up.astype(BF16), ffn2_w_down.astype(BF16))
    for layer in range(DEPTH):
        i = layer // 2
        x = _ffn(x, layer, *ffn1, final_norm, final_norm=False)
        if layer % 2 == 0:
            lam_init = 0.8 - 0.6 * math.exp(-0.3 * layer)
            qa, ka, va, qr, kr, vr, gr = _proj_ab(x, ab_norm[i], ab_w_in[i], tables, s)
            oa = _diff_attn(seq3(qa), seq3(ka), seq3(va), da_lq1[i], da_lk1[i], da_lq2[i], da_lk2[i],
                            da_norm[i], lam_init)
            orr = _retention(seq3(qr), seq3(kr), seq3(vr), seq3(gr), ret_logit_f[i], ret_logit_b[i],
                             ret_norm[i])
            mixed, w_out = [tok2(oa), tok2(orr)], ab_w_out[i].astype(BF16)
        else:
            q, k, v, gate, lr = _proj_c(x, c_norm[i], c_w_in[i])
            o = _gla(seq3(q), seq3(k), seq3(v), seq3(gate), seq3(lr), gla_w2_f[i], gla_b_f[i],
                     gla_w2_b[i], gla_b_b[i], gla_norm[i])
            mixed, w_out = [tok2(o)], c_w_out[i].astype(BF16)
        x = _ffn(x, layer, *ffn2, final_norm, final_norm=(layer == DEPTH - 1), mixed=mixed, w_out=w_out)
    return x.reshape(b, s, d)
```

```python
import functools
import math

import jax
import jax.numpy as jnp
import numpy as np
from jax import lax
from jax.experimental import pallas as pl
from jax.experimental.pallas import tpu as pltpu

D_MODEL = 1024
D_FF = 2816
DEPTH = 2
DA_HEADS = 4
DA_DIM = 64
DA_VDIM = 2 * DA_DIM
RET_HEADS = 4
RET_QK = 64
RET_V = 128
RET_CHUNK = 128
RET_THETA = 10000.0
GLA_HEADS = 4
GLA_QK = 128
GLA_V = 256
GLA_RANK = 16
GLA_TAU = 16.0
GLA_CHUNK = 64
ROPE_THETA = 500000.0
ROPE_FRAC = 4
EPS = 1e-6

LANES = 128
V7X_VMEM_LIMIT_BYTES = 56 * 1024 * 1024

F32 = jnp.float32
BF16 = jnp.bfloat16
NT_DIMS = (((1,), (1,)), ((), ()))
TN_DIMS = (((0,), (0,)), ((), ()))


def _params(n_grid_dims):
    return pltpu.CompilerParams(
        dimension_semantics=("arbitrary",) * n_grid_dims,
        vmem_limit_bytes=V7X_VMEM_LIMIT_BYTES,
    )


def _resident(shape, index_map=None):
    if index_map is None:
        index_map = lambda *_: (0,) * len(shape)
    return pl.BlockSpec(shape, index_map, pipeline_mode=pl.Buffered(1))


def _rms(x, g):
    return x * lax.rsqrt(jnp.mean(x * x, axis=-1, keepdims=True) + EPS) * g


def _silu(x):
    return x * jax.nn.sigmoid(x)


def _log_sigmoid(x):
    return jnp.minimum(x, 0.0) - jnp.log(1.0 + jnp.exp(-jnp.abs(x)))


def _dot(a, b):
    return jnp.dot(a, b, preferred_element_type=F32)


def _rope_table_kernel(pos_ref, inv_a_ref, inv_r_ref, ca_ref, sa_up_ref, sa_dn_ref,
                       cr_ref, sr_up_ref, sr_dn_ref):
    pos = pos_ref[...]
    lane = lax.broadcasted_iota(jnp.int32, (pos.shape[0], LANES), 1)
    rot = DA_DIM // ROPE_FRAC
    j = lane % DA_DIM
    ang = pos * inv_a_ref[...]
    c, s = jnp.cos(ang), jnp.sin(ang)
    ca_ref[...] = jnp.where(j < rot, c, 1.0)
    sa_up_ref[...] = jnp.where(j < rot // 2, -s, 0.0)
    sa_dn_ref[...] = jnp.where((j >= rot // 2) & (j < rot), s, 0.0)
    j = lane % RET_QK
    ang = pos * inv_r_ref[...]
    c, s = jnp.cos(ang), jnp.sin(ang)
    cr_ref[...] = c
    sr_up_ref[...] = jnp.where(j < RET_QK // 2, -s, 0.0)
    sr_dn_ref[...] = jnp.where(j >= RET_QK // 2, s, 0.0)


def _rope_tables(positions):
    seq = positions.shape[0]
    ts = 512
    rot = DA_DIM // ROPE_FRAC
    inv_a = 1.0 / (ROPE_THETA ** (jnp.arange(0, rot, 2, dtype=F32) / rot))
    inv_r = 1.0 / (RET_THETA ** (jnp.arange(0, RET_QK, 2, dtype=F32) / RET_QK))
    lane = np.arange(LANES)
    inv_a_lanes = inv_a[(lane % DA_DIM) % (rot // 2)][None, :]
    inv_r_lanes = inv_r[(lane % RET_QK) % (RET_QK // 2)][None, :]
    pos = positions.astype(F32).reshape(seq, 1)
    row = pl.BlockSpec((ts, LANES), lambda i: (i, 0))
    vec = pl.BlockSpec((1, LANES), lambda i: (0, 0))
    return pl.pallas_call(
        _rope_table_kernel,
        grid=(seq // ts,),
        in_specs=[pl.BlockSpec((ts, 1), lambda i: (i, 0)), vec, vec],
        out_specs=[row] * 6,
        out_shape=[jax.ShapeDtypeStruct((seq, LANES), F32)] * 6,
        compiler_params=_params(1),
        name="rope_tables",
    )(pos, inv_a_lanes, inv_r_lanes)


def _rope(x, c, s_up, s_dn, half):
    out = []
    for b in range(x.shape[1] // LANES):
        xb = x[:, b * LANES:(b + 1) * LANES]
        up = pltpu.roll(xb, LANES - half, 1)
        dn = pltpu.roll(xb, half, 1)
        out.append(xb * c + up * s_up + dn * s_dn)
    return jnp.concatenate(out, axis=1)


def _ffn_kernel(*refs, n_mixed, final_norm):
    x_ref, mixed = refs[0], refs[1:1 + 2 * n_mixed]
    g_ref, wg_ref, wu_ref, wd_ref, fg_ref, o_ref = refs[1 + 2 * n_mixed:]
    x = x_ref[...]
    for a_ref, w_ref in zip(mixed[::2], mixed[1::2]):
        x = x + _dot(a_ref[...], w_ref[...])
    xn = _rms(x, g_ref[...]).astype(BF16)
    gate = _dot(xn, wg_ref[...])
    up = _dot(xn, wu_ref[...])
    h = (_silu(gate) * up).astype(BF16)
    out = x + 0.5 * _dot(h, wd_ref[...])
    if final_norm:
        out = _rms(out, fg_ref[...])
    o_ref[...] = out


def _ffn(x, layer, g, wg, wu, wd, fg, *, final_norm, mixed=(), w_out=None):
    t, d = x.shape
    tm = 512
    tok = lambda wd_: pl.BlockSpec((tm, wd_), lambda i: (i, 0))
    of_layer = lambda rows, cols: _resident((None, rows, cols), lambda i: (layer, 0, 0))
    in_specs, args, row = [tok(d)], [x], 0
    for a in mixed:
        wa = a.shape[1]
        in_specs += [tok(wa), _resident((wa, d), functools.partial(lambda r, i: (r, 0), row // wa))]
        args += [a, w_out]
        row += wa
    in_specs += [of_layer(1, d), of_layer(d, D_FF), of_layer(d, D_FF), of_layer(D_FF, d), _resident((1, d))]
    args += [g.reshape(-1, 1, d), wg, wu, wd, fg.reshape(1, d)]
    return pl.pallas_call(
        functools.partial(_ffn_kernel, n_mixed=len(mixed), final_norm=final_norm),
        grid=(t // tm,),
        in_specs=in_specs,
        out_specs=tok(d),
        out_shape=jax.ShapeDtypeStruct((t, d), F32),
        compiler_params=_params(1),
        name="ffn",
    )(*args)


def _proj_ab_kernel(x_ref, g_ref, w_ref, ca_ref, sa_up_ref, sa_dn_ref, cr_ref, sr_up_ref, sr_dn_ref,
                    qa_ref, ka_ref, va_ref, qr_ref, kr_ref, vr_ref, gr_ref):
    xn = _rms(x_ref[...], g_ref[...]).astype(BF16)
    y = _dot(xn, w_ref[...])
    qa_w = DA_HEADS * 2 * DA_DIM
    va_w = DA_HEADS * DA_VDIM
    qr_w = RET_HEADS * RET_QK
    vr_w = RET_HEADS * RET_V
    o = 0
    qa, o = y[:, o:o + qa_w], o + qa_w
    ka, o = y[:, o:o + qa_w], o + qa_w
    va, o = y[:, o:o + va_w], o + va_w
    qr, o = y[:, o:o + qr_w], o + qr_w
    kr, o = y[:, o:o + qr_w], o + qr_w
    vr, o = y[:, o:o + vr_w], o + vr_w
    gr = y[:, o:o + vr_w]
    rope_a = functools.partial(_rope, c=ca_ref[...], s_up=sa_up_ref[...], s_dn=sa_dn_ref[...],
                               half=DA_DIM // ROPE_FRAC // 2)
    rope_r = functools.partial(_rope, c=cr_ref[...], s_up=sr_up_ref[...], s_dn=sr_dn_ref[...],
                               half=RET_QK // 2)
    qa_ref[...] = (rope_a(qa) * (DA_DIM ** -0.5 * math.log2(math.e))).astype(BF16)
    ka_ref[...] = rope_a(ka).astype(BF16)
    va_ref[...] = va.astype(BF16)
    qr_ref[...] = rope_r(qr)
    kr_ref[...] = rope_r(kr) * RET_QK ** -0.5
    vr_ref[...] = vr.astype(BF16)
    gr_ref[...] = gr


def _proj_ab(x, g, w, tables, seq):
    t, d = x.shape
    tm = 512
    n_in = w.shape[1]
    widths = (DA_HEADS * 2 * DA_DIM, DA_HEADS * 2 * DA_DIM, DA_HEADS * DA_VDIM,
              RET_HEADS * RET_QK, RET_HEADS * RET_QK, RET_HEADS * RET_V, RET_HEADS * RET_V)
    dtypes = (BF16, BF16, BF16, F32, F32, BF16, F32)
    tok = lambda wd: pl.BlockSpec((tm, wd), lambda i: (i, 0))
    tab = pl.BlockSpec((tm, LANES), lambda i: (i % (seq // tm), 0))
    return pl.pallas_call(
        _proj_ab_kernel,
        grid=(t // tm,),
        in_specs=[tok(d), _resident((1, d)), _resident((d, n_in))] + [tab] * 6,
        out_specs=[tok(wd) for wd in widths],
        out_shape=[jax.ShapeDtypeStruct((t, wd), dt) for wd, dt in zip(widths, dtypes)],
        compiler_params=_params(1),
        name="proj_ab",
    )(x, g.reshape(1, d), w.astype(BF16), *tables)


PROJ_SLAB = 256


def _proj_c_kernel(x_ref, g_ref, w_ref, wlr_ref, w2f_ref, w2b_ref, bf_ref, bb_ref,
                   q_ref, k_ref, v_ref, gate_ref, cumf_ref, cumb_ref):
    xn = _rms(x_ref[...], g_ref[...]).astype(BF16)
    lr = _dot(xn, wlr_ref[...]).astype(BF16)
    qk_w = GLA_HEADS * GLA_QK
    v_w = GLA_HEADS * GLA_V
    outs = ((q_ref, 0, lambda y: y * GLA_QK ** -0.5), (k_ref, qk_w, lambda y: y),
            (v_ref, 2 * qk_w, lambda y: y.astype(BF16)), (gate_ref, 2 * qk_w + v_w, lambda y: y))
    tm, n_main = x_ref.shape[0], w_ref.shape[1]

    def slab(lo):
        hi = lo + PROJ_SLAB
        y = _dot(xn, w_ref[:, lo:hi])
        for ref, first, post in outs:
            a, b = max(lo, first), min(hi, first + ref.shape[1])
            if a < b:
                ref[:, a - first:b - first] = post(y[:, a - lo:b - lo])

    def gates(h, direction, rows):
        w2_ref, b_ref, cum_ref, reverse = direction
        cols = slice(h * GLA_QK, (h + 1) * GLA_QK)
        log_a = _log_sigmoid(_dot(lr[rows], w2_ref[:, cols]) + b_ref[:, cols]) / GLA_TAU
        cum_ref[rows, cols] = _chunk_cumsum(log_a, reverse)

    directions = ((w2f_ref, bf_ref, cumf_ref, False), (w2b_ref, bb_ref, cumb_ref, True))
    row_blocks = [slice(r, r + tm // 2) for r in (0, tm // 2)]
    mxu_work = [functools.partial(slab, lo) for lo in range(0, n_main, PROJ_SLAB)]
    vpu_work = [functools.partial(gates, h, d, rows) for h in range(GLA_HEADS) for d in directions
                for rows in row_blocks]
    order = sorted([(i / len(mxu_work), 0, f) for i, f in enumerate(mxu_work)]
                   + [(i / len(vpu_work), 1, f) for i, f in enumerate(vpu_work)], key=lambda e: e[:2])
    for _, _, work in order:
        work()


def _proj_c(x, g, w, w2_f, b_f, w2_b, b_b):
    t, d = x.shape
    tm = 512
    qk_w = GLA_HEADS * GLA_QK
    v_w = GLA_HEADS * GLA_V
    n_main = 2 * qk_w + 2 * v_w
    w = w.astype(BF16)
    w_lr = jnp.pad(w[:, n_main:], ((0, 0), (0, LANES - 2 * GLA_RANK)))
    w2f = jnp.zeros((LANES, qk_w), BF16).at[:GLA_RANK].set(w2_f.astype(BF16))
    w2b = jnp.zeros((LANES, qk_w), BF16).at[GLA_RANK:2 * GLA_RANK].set(w2_b.astype(BF16))
    widths = (qk_w, qk_w, v_w, v_w, qk_w, qk_w)
    dtypes = (F32, F32, BF16, F32, F32, F32)
    tok = lambda wd: pl.BlockSpec((tm, wd), lambda i: (i, 0))
    return pl.pallas_call(
        _proj_c_kernel,
        grid=(t // tm,),
        in_specs=[tok(d), _resident((1, d)), _resident((d, n_main)), _resident((d, LANES)),
                  _resident((LANES, qk_w)), _resident((LANES, qk_w)), _resident((1, qk_w)),
                  _resident((1, qk_w))],
        out_specs=[tok(wd) for wd in widths],
        out_shape=[jax.ShapeDtypeStruct((t, wd), dt) for wd, dt in zip(widths, dtypes)],
        compiler_params=_params(1),
        name="proj_c",
    )(x, g.reshape(1, d), w[:, :n_main], w_lr, w2f, w2b, b_f.reshape(1, -1), b_b.reshape(1, -1))


DA_KEY_TILE = 256


def _diff_attn_kernel(q_ref, k_ref, v_ref, lq1_ref, lk1_ref, lq2_ref, lk2_ref, g_ref, o_ref, *, lam_init):
    q = q_ref[0]
    lane = lax.broadcasted_iota(jnp.int32, q.shape, 1)
    zero = jnp.zeros_like(q)
    q_comp = (jnp.where(lane < DA_DIM, q, zero), jnp.where(lane >= DA_DIM, q, zero))
    scores = [lax.dot_general(qc, k_ref[0], NT_DIMS, preferred_element_type=F32) for qc in q_comp]
    row_max = [jnp.max(s, axis=-1, keepdims=True) for s in scores]
    tq, seq = scores[0].shape
    ones = jnp.ones((DA_KEY_TILE, LANES), BF16)
    pv = [jnp.zeros((tq, DA_VDIM + LANES), F32) for _ in scores]
    for j in range(seq // DA_KEY_TILE):
        cols = slice(j * DA_KEY_TILE, (j + 1) * DA_KEY_TILE)
        v_ones = jnp.concatenate([v_ref[0, cols, :], ones], axis=1)
        for c, s in enumerate(scores):
            pj = jnp.exp2(s[:, cols] - row_max[c]).astype(BF16)
            pv[c] = pv[c] + _dot(pj, v_ones)
    lam = (jnp.exp(jnp.sum(lq1_ref[...] * lk1_ref[...], axis=-1, keepdims=True))
           - jnp.exp(jnp.sum(lq2_ref[...] * lk2_ref[...], axis=-1, keepdims=True)) + lam_init)
    soft = [x[:, :DA_VDIM] / x[:, DA_VDIM:] for x in pv]
    o = soft[0] - lam * soft[1]
    o_ref[0] = (_rms(o, g_ref[...]) * (1.0 - lam_init)).astype(BF16)


def _diff_attn(q, k, v, lq1, lk1, lq2, lk2, g, lam_init):
    b, s, _ = q.shape
    tq = 1024
    hw = 2 * DA_DIM
    vec = lambda n: _resident((1, n))
    return pl.pallas_call(
        functools.partial(_diff_attn_kernel, lam_init=lam_init),
        grid=(b, DA_HEADS, s // tq),
        in_specs=[pl.BlockSpec((1, tq, hw), lambda bi, h, qi: (bi, qi, h)),
                  pl.BlockSpec((1, s, hw), lambda bi, h, qi: (bi, 0, h)),
                  pl.BlockSpec((1, s, DA_VDIM), lambda bi, h, qi: (bi, 0, h)),
                  vec(DA_DIM), vec(DA_DIM), vec(DA_DIM), vec(DA_DIM), vec(DA_VDIM)],
        out_specs=pl.BlockSpec((1, tq, DA_VDIM), lambda bi, h, qi: (bi, qi, h)),
        out_shape=jax.ShapeDtypeStruct((b, s, DA_HEADS * DA_VDIM), BF16),
        compiler_params=_params(3),
        name="diff_attn",
    )(q, k, v, lq1.reshape(1, -1), lk1.reshape(1, -1), lq2.reshape(1, -1), lk2.reshape(1, -1),
      g.reshape(1, -1))


RET_PAIR = LANES // RET_QK


def _retention_kernel(q_ref, k_ref, v_ref, gate_ref, lf_ref, lb_ref, g_ref, o_ref, st_ref):
    c = RET_CHUNK
    n_chunks = q_ref.shape[1] // c
    pw = RET_PAIR * RET_V
    row = lax.broadcasted_iota(jnp.int32, (c, c), 0).astype(F32)
    col = lax.broadcasted_iota(jnp.int32, (c, c), 1).astype(F32)
    lane_head = lax.broadcasted_iota(jnp.int32, (c, LANES), 1) // RET_QK
    rows = lambda n: pl.ds(pl.multiple_of(n * c, c), c)
    heads = range(RET_PAIR)
    lg_f = [_log_sigmoid(lf_ref[j]) for j in heads]
    lg_b = [_log_sigmoid(lb_ref[j]) for j in heads]

    def per_lane(lg):
        out = lg[0]
        for j in heads[1:]:
            out = jnp.where(lane_head[:1] == j, lg[j], out)
        return out

    per_col = lambda lg: jnp.concatenate([jnp.broadcast_to(x, (1, RET_V)) for x in lg], axis=1)
    diff = row - col
    decay = [jnp.where(diff >= 0, jnp.exp(lg_f[j] * jnp.maximum(diff, 0.0)), 0.0)
             + jnp.where(diff <= 0, jnp.exp(lg_b[j] * jnp.maximum(-diff, 0.0)), 0.0) for j in heads]
    lgl_f, lgl_b = per_lane(lg_f), per_lane(lg_b)
    q_dec_f = jnp.exp(lgl_f * (row + 1.0))
    k_dec_f = jnp.exp(lgl_f * (c - 1.0 - row))
    q_dec_b = jnp.exp(lgl_b * (c - row))
    k_dec_b = jnp.exp(lgl_b * row)
    chunk_dec_f = jnp.exp(per_col(lg_f) * c)
    chunk_dec_b = jnp.exp(per_col(lg_b) * c)
    own_block = (lax.broadcasted_iota(jnp.int32, (LANES, pw), 0) // RET_QK
                 == lax.broadcasted_iota(jnp.int32, (LANES, pw), 1) // RET_V)

    def kv(n, k_dec):
        kd = (k_ref[0, rows(n), :] * k_dec).astype(BF16)
        out = lax.dot_general(kd, v_ref[0, rows(n), :], TN_DIMS, preferred_element_type=F32)
        return jnp.where(own_block, out, 0.0)

    def scan(i, states):
        sf, sb = states
        nf, nb = i, n_chunks - 1 - i
        st_ref[nf, :LANES, :] = sf.astype(BF16)
        st_ref[nb, LANES:, :] = sb.astype(BF16)
        return chunk_dec_f * sf + kv(nf, k_dec_f), chunk_dec_b * sb + kv(nb, k_dec_b)

    zero = jnp.zeros((LANES, pw), F32)
    lax.fori_loop(0, n_chunks, scan, (zero, zero), unroll=8)

    def out(n, carry):
        qc = q_ref[0, rows(n), :]
        kc = k_ref[0, rows(n), :].astype(BF16)
        vc = v_ref[0, rows(n), :]
        q_heads = jnp.concatenate([jnp.where(lane_head == j, qc, 0.0) for j in heads], axis=0)
        sc = lax.dot_general(q_heads.astype(BF16), kc, NT_DIMS, preferred_element_type=F32)
        inter = _dot(jnp.concatenate([qc * q_dec_f, qc * q_dec_b], axis=1).astype(BF16), st_ref[n])
        o = []
        for j in heads:
            a = (sc[j * c:(j + 1) * c] * decay[j]).astype(BF16)
            oj = _dot(a, vc[:, j * RET_V:(j + 1) * RET_V]) + inter[:, j * RET_V:(j + 1) * RET_V]
            o.append(_rms(oj, g_ref[...]))
        o = jnp.concatenate(o, axis=1) * _silu(gate_ref[0, rows(n), :])
        o_ref[0, rows(n), :] = o.astype(BF16)
        return carry

    lax.fori_loop(0, n_chunks, out, 0, unroll=8)


def _retention(q, k, v, gate, logit_f, logit_b, g):
    b, s, _ = q.shape
    n_chunks = s // RET_CHUNK
    pw = RET_PAIR * RET_V
    lanes = lambda x: jnp.broadcast_to(x.astype(F32)[:, None, None], (RET_HEADS, 1, LANES))
    qk = pl.BlockSpec((1, s, LANES), lambda bi, p: (bi, 0, p))
    vg = pl.BlockSpec((1, s, pw), lambda bi, p: (bi, 0, p))
    logit = pl.BlockSpec((RET_PAIR, 1, LANES), lambda bi, p: (p, 0, 0))
    return pl.pallas_call(
        _retention_kernel,
        grid=(b, RET_HEADS // RET_PAIR),
        in_specs=[qk, qk, vg, vg, logit, logit, _resident((1, RET_V))],
        out_specs=vg,
        out_shape=jax.ShapeDtypeStruct((b, s, RET_HEADS * RET_V), BF16),
        scratch_shapes=[pltpu.VMEM((n_chunks, 2 * LANES, pw), BF16)],
        compiler_params=_params(2),
        name="retention",
    )(q, k, v, gate, lanes(logit_f), lanes(logit_b), g.reshape(1, -1))


GLA_GROUP = 4


def _chunk_cumsum(x, reverse):
    n = x.shape[0]
    pos = lax.broadcasted_iota(jnp.int32, x.shape, 0) % GLA_CHUNK
    step = 1
    while step < GLA_CHUNK:
        if reverse:
            x = x + jnp.where(pos < GLA_CHUNK - step, pltpu.roll(x, n - step, 0), 0.0)
        else:
            x = x + jnp.where(pos >= step, pltpu.roll(x, step, 0), 0.0)
        step *= 2
    return x


def _gla_kernel(q_ref, k_ref, v_ref, gate_ref, cumf_ref, cumb_ref, g_ref, o_ref, stf_ref, stb_ref):
    c = GLA_CHUNK
    n_chunks = q_ref.shape[1] // c
    rows = lambda n: pl.ds(pl.multiple_of(n * c, c), c)
    cumf_ref, cumb_ref = cumf_ref.at[0], cumb_ref.at[0]
    gsz = GLA_GROUP * c
    row = lax.broadcasted_iota(jnp.int32, (gsz, gsz), 0)
    col = lax.broadcasted_iota(jnp.int32, (gsz, gsz), 1)
    same_chunk = (row // c) == (col // c)
    fwd = (cumf_ref, stf_ref, c - 1, same_chunk & (row >= col))
    bwd = (cumb_ref, stb_ref, 0, same_chunk & (row <= col))

    def step(n, state, direction):
        cum_ref, st_ref, last_row, _ = direction
        st_ref[n] = state.astype(BF16)
        cum = cum_ref[rows(n), :]
        last = cum[last_row:last_row + 1, :]
        kd = (k_ref[0, rows(n), :] * jnp.exp(last - cum)).astype(BF16)
        kv = lax.dot_general(v_ref[0, rows(n), :], kd, TN_DIMS, preferred_element_type=F32)
        return jnp.exp(last) * state + kv

    def scan(i, states):
        return step(i, states[0], fwd), step(n_chunks - 1 - i, states[1], bwd)

    zero = jnp.zeros((GLA_V, GLA_QK), F32)
    lax.fori_loop(0, n_chunks, scan, (zero, zero), unroll=8)

    def attend(g, direction):
        cum_ref, st_ref, _, keep = direction
        grows = pl.ds(pl.multiple_of(g * gsz, gsz), gsz)
        cum = cum_ref[grows, :]
        qg = (q_ref[0, grows, :] * jnp.exp(cum)).astype(BF16)
        kk = (k_ref[0, grows, :] * jnp.exp(-cum)).astype(BF16)
        sc = lax.dot_general(qg, kk, NT_DIMS, preferred_element_type=F32)
        intra = _dot(jnp.where(keep, sc, 0.0).astype(BF16), v_ref[0, grows, :])
        inter = [lax.dot_general(qg[i * c:(i + 1) * c], st_ref[g * GLA_GROUP + i], NT_DIMS,
                                 preferred_element_type=F32) for i in range(GLA_GROUP)]
        return intra + jnp.concatenate(inter, axis=0)

    def out(g, carry):
        grows = pl.ds(pl.multiple_of(g * gsz, gsz), gsz)
        o = attend(g, fwd) + attend(g, bwd)
        o_ref[0, grows, :] = (_rms(o, g_ref[...]) * _silu(gate_ref[0, grows, :])).astype(BF16)
        return carry

    lax.fori_loop(0, n_chunks // GLA_GROUP, out, 0, unroll=4)


def _gla(q, k, v, gate, cum_f, cum_b, g):
    b, s, _ = q.shape
    qk = pl.BlockSpec((1, s, GLA_QK), lambda bi, h: (bi, 0, h))
    vg = pl.BlockSpec((1, s, GLA_V), lambda bi, h: (bi, 0, h))
    return pl.pallas_call(
        _gla_kernel,
        grid=(b, GLA_HEADS),
        in_specs=[qk, qk, vg, vg, qk, qk, _resident((1, GLA_V))],
        out_specs=vg,
        out_shape=jax.ShapeDtypeStruct((b, s, GLA_HEADS * GLA_V), BF16),
        scratch_shapes=[pltpu.VMEM((s // GLA_CHUNK, GLA_V, GLA_QK), BF16),
                        pltpu.VMEM((s // GLA_CHUNK, GLA_V, GLA_QK), BF16)],
        compiler_params=_params(2),
        name="gla",
    )(q, k, v, gate, cum_f, cum_b, g.reshape(1, -1))


def kernel(x, positions, ffn1_norm, ffn1_w_gate, ffn1_w_up, ffn1_w_down, ffn2_norm, ffn2_w_gate, ffn2_w_up, ffn2_w_down, ab_norm, ab_w_in, da_lq1, da_lk1, da_lq2, da_lk2, da_norm, ret_logit_f, ret_logit_b, ret_norm, ab_w_out, c_norm, c_w_in, gla_w2_f, gla_b_f, gla_w2_b, gla_b_b, gla_norm, c_w_out, final_norm):
    b, s, d = x.shape
    t = b * s
    tables = _rope_tables(positions)
    x = x.reshape(t, d)
    seq3 = lambda a: a.reshape(b, s, -1)
    tok2 = lambda a: a.reshape(t, -1)
    ffn1 = (ffn1_norm, ffn1_w_gate.astype(BF16), ffn1_w_up.astype(BF16), ffn1_w_down.astype(BF16))
    ffn2 = (ffn2_norm, ffn2_w_gate.astype(BF16), ffn2_w_up.astype(BF16), ffn2_w_down.astype(BF16))
    for layer in range(DEPTH):
        i = layer // 2
        x = _ffn(x, layer, *ffn1, final_norm, final_norm=False)
        if layer % 2 == 0:
            lam_init = 0.8 - 0.6 * math.exp(-0.3 * layer)
            qa, ka, va, qr, kr, vr, gr = _proj_ab(x, ab_norm[i], ab_w_in[i], tables, s)
            oa = _diff_attn(seq3(qa), seq3(ka), seq3(va), da_lq1[i], da_lk1[i], da_lq2[i], da_lk2[i],
                            da_norm[i], lam_init)
            orr = _retention(seq3(qr), seq3(kr), seq3(vr), seq3(gr), ret_logit_f[i], ret_logit_b[i],
                             ret_norm[i])
            mixed, w_out = [tok2(oa), tok2(orr)], ab_w_out[i].astype(BF16)
        else:
            q, k, v, gate, cum_f, cum_b = _proj_c(x, c_norm[i], c_w_in[i], gla_w2_f[i], gla_b_f[i],
                                                  gla_w2_b[i], gla_b_b[i])
            o = _gla(seq3(q), seq3(k), seq3(v), seq3(gate), seq3(cum_f), seq3(cum_b), gla_norm[i])
            mixed, w_out = [tok2(o)], c_w_out[i].astype(BF16)
        x = _ffn(x, layer, *ffn2, final_norm, final_norm=(layer == DEPTH - 1), mixed=mixed, w_out=w_out)
    return x.reshape(b, s, d)
```

```python
import functools
import math

import jax
import jax.numpy as jnp
import numpy as np
from jax import lax
from jax.experimental import pallas as pl
from jax.experimental.pallas import tpu as pltpu

D_MODEL = 1024
D_FF = 2816
DEPTH = 2
DA_HEADS = 4
DA_DIM = 64
DA_VDIM = 2 * DA_DIM
RET_HEADS = 4
RET_QK = 64
RET_V = 128
RET_CHUNK = 128
RET_THETA = 10000.0
GLA_HEADS = 4
GLA_QK = 128
GLA_V = 256
GLA_RANK = 16
GLA_TAU = 16.0
GLA_CHUNK = 64
ROPE_THETA = 500000.0
ROPE_FRAC = 4
EPS = 1e-6

LANES = 128
SUBLANES = 8
V7X_VMEM_LIMIT_BYTES = 56 * 1024 * 1024

F32 = jnp.float32
BF16 = jnp.bfloat16
NT_DIMS = (((1,), (1,)), ((), ()))
TN_DIMS = (((0,), (0,)), ((), ()))


def _params(n_grid_dims):
    return pltpu.CompilerParams(
        dimension_semantics=("arbitrary",) * n_grid_dims,
        vmem_limit_bytes=V7X_VMEM_LIMIT_BYTES,
    )


def _resident(shape, index_map=None):
    if index_map is None:
        index_map = lambda *_: (0,) * len(shape)
    return pl.BlockSpec(shape, index_map, pipeline_mode=pl.Buffered(1))


def _rms(x, g):
    return x * lax.rsqrt(jnp.mean(x * x, axis=-1, keepdims=True) + EPS) * g


def _silu(x):
    return x * jax.nn.sigmoid(x)


def _log_sigmoid(x):
    return jnp.minimum(x, 0.0) - jnp.log(1.0 + jnp.exp(-jnp.abs(x)))


def _dot(a, b):
    return jnp.dot(a, b, preferred_element_type=F32)


def _rope_table_kernel(pos_ref, inv_a_ref, inv_r_ref, ca_ref, sa_up_ref, sa_dn_ref,
                       cr_ref, sr_up_ref, sr_dn_ref):
    pos = pos_ref[...]
    lane = lax.broadcasted_iota(jnp.int32, (pos.shape[0], LANES), 1)
    rot = DA_DIM // ROPE_FRAC
    j = lane % DA_DIM
    ang = pos * inv_a_ref[...]
    c, s = jnp.cos(ang), jnp.sin(ang)
    ca_ref[...] = jnp.where(j < rot, c, 1.0)
    sa_up_ref[...] = jnp.where(j < rot // 2, -s, 0.0)
    sa_dn_ref[...] = jnp.where((j >= rot // 2) & (j < rot), s, 0.0)
    j = lane % RET_QK
    ang = pos * inv_r_ref[...]
    c, s = jnp.cos(ang), jnp.sin(ang)
    cr_ref[...] = c
    sr_up_ref[...] = jnp.where(j < RET_QK // 2, -s, 0.0)
    sr_dn_ref[...] = jnp.where(j >= RET_QK // 2, s, 0.0)


def _rope_tables(positions):
    seq = positions.shape[0]
    ts = 512
    rot = DA_DIM // ROPE_FRAC
    inv_a = 1.0 / (ROPE_THETA ** (jnp.arange(0, rot, 2, dtype=F32) / rot))
    inv_r = 1.0 / (RET_THETA ** (jnp.arange(0, RET_QK, 2, dtype=F32) / RET_QK))
    lane = np.arange(LANES)
    inv_a_lanes = inv_a[(lane % DA_DIM) % (rot // 2)][None, :]
    inv_r_lanes = inv_r[(lane % RET_QK) % (RET_QK // 2)][None, :]
    pos = positions.astype(F32).reshape(seq, 1)
    row = pl.BlockSpec((ts, LANES), lambda i: (i, 0))
    vec = pl.BlockSpec((1, LANES), lambda i: (0, 0))
    return pl.pallas_call(
        _rope_table_kernel,
        grid=(seq // ts,),
        in_specs=[pl.BlockSpec((ts, 1), lambda i: (i, 0)), vec, vec],
        out_specs=[row] * 6,
        out_shape=[jax.ShapeDtypeStruct((seq, LANES), F32)] * 6,
        compiler_params=_params(1),
        name="rope_tables",
    )(pos, inv_a_lanes, inv_r_lanes)


def _rope(x, c, s_up, s_dn, half):
    out = []
    for b in range(x.shape[1] // LANES):
        xb = x[:, b * LANES:(b + 1) * LANES]
        up = pltpu.roll(xb, LANES - half, 1)
        dn = pltpu.roll(xb, half, 1)
        out.append(xb * c + up * s_up + dn * s_dn)
    return jnp.concatenate(out, axis=1)


def _ffn_kernel(*refs, n_mixed, final_norm):
    x_ref, mixed = refs[0], refs[1:1 + 2 * n_mixed]
    g_ref, wg_ref, wu_ref, wd_ref, fg_ref, o_ref = refs[1 + 2 * n_mixed:]
    x = x_ref[...]
    for a_ref, w_ref in zip(mixed[::2], mixed[1::2]):
        x = x + _dot(a_ref[...], w_ref[...])
    xn = _rms(x, g_ref[...]).astype(BF16)
    gate = _dot(xn, wg_ref[...])
    up = _dot(xn, wu_ref[...])
    h = (_silu(gate) * up).astype(BF16)
    out = x + 0.5 * _dot(h, wd_ref[...])
    if final_norm:
        out = _rms(out, fg_ref[...])
    o_ref[...] = out


def _ffn(x, layer, g, wg, wu, wd, fg, *, final_norm, mixed=(), w_out=None):
    t, d = x.shape
    tm = 512
    tok = lambda wd_: pl.BlockSpec((tm, wd_), lambda i: (i, 0))
    of_layer = lambda rows, cols: _resident((None, rows, cols), lambda i: (layer, 0, 0))
    in_specs, args, row = [tok(d)], [x], 0
    for a in mixed:
        wa = a.shape[1]
        in_specs += [tok(wa), _resident((wa, d), functools.partial(lambda r, i: (r, 0), row // wa))]
        args += [a, w_out]
        row += wa
    in_specs += [of_layer(1, d), of_layer(d, D_FF), of_layer(d, D_FF), of_layer(D_FF, d), _resident((1, d))]
    args += [g.reshape(-1, 1, d), wg, wu, wd, fg.reshape(1, d)]
    return pl.pallas_call(
        functools.partial(_ffn_kernel, n_mixed=len(mixed), final_norm=final_norm),
        grid=(t // tm,),
        in_specs=in_specs,
        out_specs=tok(d),
        out_shape=jax.ShapeDtypeStruct((t, d), F32),
        compiler_params=_params(1),
        name="ffn",
    )(*args)


def _proj_ab_kernel(x_ref, g_ref, w_ref, ca_ref, sa_up_ref, sa_dn_ref, cr_ref, sr_up_ref, sr_dn_ref,
                    qa_ref, ka_ref, va_ref, qr_ref, kr_ref, vr_ref, gr_ref):
    xn = _rms(x_ref[...], g_ref[...]).astype(BF16)
    y = _dot(xn, w_ref[...])
    qa_w = DA_HEADS * 2 * DA_DIM
    va_w = DA_HEADS * DA_VDIM
    qr_w = RET_HEADS * RET_QK
    vr_w = RET_HEADS * RET_V
    o = 0
    qa, o = y[:, o:o + qa_w], o + qa_w
    ka, o = y[:, o:o + qa_w], o + qa_w
    va, o = y[:, o:o + va_w], o + va_w
    qr, o = y[:, o:o + qr_w], o + qr_w
    kr, o = y[:, o:o + qr_w], o + qr_w
    vr, o = y[:, o:o + vr_w], o + vr_w
    gr = y[:, o:o + vr_w]
    rope_a = functools.partial(_rope, c=ca_ref[...], s_up=sa_up_ref[...], s_dn=sa_dn_ref[...],
                               half=DA_DIM // ROPE_FRAC // 2)
    rope_r = functools.partial(_rope, c=cr_ref[...], s_up=sr_up_ref[...], s_dn=sr_dn_ref[...],
                               half=RET_QK // 2)
    qa_ref[...] = (rope_a(qa) * (DA_DIM ** -0.5 * math.log2(math.e))).astype(BF16)
    ka_ref[...] = rope_a(ka).astype(BF16)
    va_ref[...] = va.astype(BF16)
    qr_ref[...] = rope_r(qr)
    kr_ref[...] = rope_r(kr) * RET_QK ** -0.5
    vr_ref[...] = vr.astype(BF16)
    gr_ref[...] = gr


def _proj_ab(x, g, w, tables, seq):
    t, d = x.shape
    tm = 1024
    n_in = w.shape[1]
    widths = (DA_HEADS * 2 * DA_DIM, DA_HEADS * 2 * DA_DIM, DA_HEADS * DA_VDIM,
              RET_HEADS * RET_QK, RET_HEADS * RET_QK, RET_HEADS * RET_V, RET_HEADS * RET_V)
    dtypes = (BF16, BF16, BF16, F32, F32, BF16, F32)
    tok = lambda wd: pl.BlockSpec((tm, wd), lambda i: (i, 0))
    tab = pl.BlockSpec((tm, LANES), lambda i: (i % (seq // tm), 0))
    return pl.pallas_call(
        _proj_ab_kernel,
        grid=(t // tm,),
        in_specs=[tok(d), _resident((1, d)), _resident((d, n_in))] + [tab] * 6,
        out_specs=[tok(wd) for wd in widths],
        out_shape=[jax.ShapeDtypeStruct((t, wd), dt) for wd, dt in zip(widths, dtypes)],
        compiler_params=_params(1),
        name="proj_ab",
    )(x, g.reshape(1, d), w.astype(BF16), *tables)


PROJ_SLAB = 256


def _proj_c_kernel(x_ref, g_ref, w_ref, wlr_ref, w2f_ref, w2b_ref, bf_ref, bb_ref,
                   q_ref, k_ref, v_ref, gate_ref, cumf_ref, cumb_ref):
    xn = _rms(x_ref[...], g_ref[...]).astype(BF16)
    lr = _dot(xn, wlr_ref[...]).astype(BF16)
    qk_w = GLA_HEADS * GLA_QK
    v_w = GLA_HEADS * GLA_V
    outs = ((q_ref, 0, lambda y: y * GLA_QK ** -0.5), (k_ref, qk_w, lambda y: y),
            (v_ref, 2 * qk_w, lambda y: y.astype(BF16)), (gate_ref, 2 * qk_w + v_w, lambda y: y))
    tm, n_main = x_ref.shape[0], w_ref.shape[1]

    def slab(lo):
        hi = lo + PROJ_SLAB
        y = _dot(xn, w_ref[:, lo:hi])
        for ref, first, post in outs:
            a, b = max(lo, first), min(hi, first + ref.shape[1])
            if a < b:
                ref[:, a - first:b - first] = post(y[:, a - lo:b - lo])

    def gates(h, direction, rows):
        w2_ref, b_ref, cum_ref, reverse = direction
        cols = slice(h * GLA_QK, (h + 1) * GLA_QK)
        log_a = _log_sigmoid(_dot(lr[rows], w2_ref[:, cols]) + b_ref[:, cols]) / GLA_TAU
        cum_ref[rows, cols] = _chunk_cumsum(log_a, reverse)

    directions = ((w2f_ref, bf_ref, cumf_ref, False), (w2b_ref, bb_ref, cumb_ref, True))
    row_blocks = [slice(r, r + tm // 2) for r in (0, tm // 2)]
    mxu_work = [functools.partial(slab, lo) for lo in range(0, n_main, PROJ_SLAB)]
    vpu_work = [functools.partial(gates, h, d, rows) for h in range(GLA_HEADS) for d in directions
                for rows in row_blocks]
    order = sorted([(i / len(mxu_work), 0, f) for i, f in enumerate(mxu_work)]
                   + [(i / len(vpu_work), 1, f) for i, f in enumerate(vpu_work)], key=lambda e: e[:2])
    for _, _, work in order:
        work()


def _proj_c(x, g, w, w2_f, b_f, w2_b, b_b):
    t, d = x.shape
    tm = 512
    qk_w = GLA_HEADS * GLA_QK
    v_w = GLA_HEADS * GLA_V
    n_main = 2 * qk_w + 2 * v_w
    w = w.astype(BF16)
    w_lr = jnp.pad(w[:, n_main:], ((0, 0), (0, LANES - 2 * GLA_RANK)))
    w2f = jnp.zeros((LANES, qk_w), BF16).at[:GLA_RANK].set(w2_f.astype(BF16))
    w2b = jnp.zeros((LANES, qk_w), BF16).at[GLA_RANK:2 * GLA_RANK].set(w2_b.astype(BF16))
    widths = (qk_w, qk_w, v_w, v_w, qk_w, qk_w)
    dtypes = (F32, F32, BF16, F32, F32, F32)
    tok = lambda wd: pl.BlockSpec((tm, wd), lambda i: (i, 0))
    return pl.pallas_call(
        _proj_c_kernel,
        grid=(t // tm,),
        in_specs=[tok(d), _resident((1, d)), _resident((d, n_main)), _resident((d, LANES)),
                  _resident((LANES, qk_w)), _resident((LANES, qk_w)), _resident((1, qk_w)),
                  _resident((1, qk_w))],
        out_specs=[tok(wd) for wd in widths],
        out_shape=[jax.ShapeDtypeStruct((t, wd), dt) for wd, dt in zip(widths, dtypes)],
        compiler_params=_params(1),
        name="proj_c",
    )(x, g.reshape(1, d), w[:, :n_main], w_lr, w2f, w2b, b_f.reshape(1, -1), b_b.reshape(1, -1))


DA_KEY_TILE = 256


def _diff_attn_kernel(q_ref, k_ref, v_ref, lq1_ref, lk1_ref, lq2_ref, lk2_ref, g_ref, o_ref, *, lam_init):
    q = q_ref[0]
    lane = lax.broadcasted_iota(jnp.int32, q.shape, 1)
    zero = jnp.zeros_like(q)
    q_comp = (jnp.where(lane < DA_DIM, q, zero), jnp.where(lane >= DA_DIM, q, zero))
    scores = [lax.dot_general(qc, k_ref[0], NT_DIMS, preferred_element_type=F32) for qc in q_comp]
    row_max = [jnp.max(s, axis=-1, keepdims=True) for s in scores]
    tq, seq = scores[0].shape
    ones = jnp.ones((DA_KEY_TILE, LANES), BF16)
    pv = [jnp.zeros((tq, DA_VDIM + LANES), F32) for _ in scores]
    for j in range(seq // DA_KEY_TILE):
        cols = slice(j * DA_KEY_TILE, (j + 1) * DA_KEY_TILE)
        v_ones = jnp.concatenate([v_ref[0, cols, :], ones], axis=1)
        for c, s in enumerate(scores):
            pj = jnp.exp2(s[:, cols] - row_max[c]).astype(BF16)
            pv[c] = pv[c] + _dot(pj, v_ones)
    lam = (jnp.exp(jnp.sum(lq1_ref[...] * lk1_ref[...], axis=-1, keepdims=True))
           - jnp.exp(jnp.sum(lq2_ref[...] * lk2_ref[...], axis=-1, keepdims=True)) + lam_init)
    soft = [x[:, :DA_VDIM] / x[:, DA_VDIM:] for x in pv]
    o = soft[0] - lam * soft[1]
    o_ref[0] = (_rms(o, g_ref[...]) * (1.0 - lam_init)).astype(BF16)


def _diff_attn(q, k, v, lq1, lk1, lq2, lk2, g, lam_init):
    b, s, _ = q.shape
    tq = 1024
    hw = 2 * DA_DIM
    vec = lambda n: _resident((1, n))
    return pl.pallas_call(
        functools.partial(_diff_attn_kernel, lam_init=lam_init),
        grid=(b, DA_HEADS, s // tq),
        in_specs=[pl.BlockSpec((1, tq, hw), lambda bi, h, qi: (bi, qi, h)),
                  pl.BlockSpec((1, s, hw), lambda bi, h, qi: (bi, 0, h)),
                  pl.BlockSpec((1, s, DA_VDIM), lambda bi, h, qi: (bi, 0, h)),
                  vec(DA_DIM), vec(DA_DIM), vec(DA_DIM), vec(DA_DIM), vec(DA_VDIM)],
        out_specs=pl.BlockSpec((1, tq, DA_VDIM), lambda bi, h, qi: (bi, qi, h)),
        out_shape=jax.ShapeDtypeStruct((b, s, DA_HEADS * DA_VDIM), BF16),
        compiler_params=_params(3),
        name="diff_attn",
    )(q, k, v, lq1.reshape(1, -1), lk1.reshape(1, -1), lq2.reshape(1, -1), lk2.reshape(1, -1),
      g.reshape(1, -1))


RET_PAIR = LANES // RET_QK


def _retention_kernel(q_ref, k_ref, v_ref, gate_ref, lf_ref, lb_ref, g_ref, o_ref, st_ref):
    c = RET_CHUNK
    n_chunks = q_ref.shape[1] // c
    pw = RET_PAIR * RET_V
    row = lax.broadcasted_iota(jnp.int32, (c, c), 0).astype(F32)
    col = lax.broadcasted_iota(jnp.int32, (c, c), 1).astype(F32)
    lane_head = lax.broadcasted_iota(jnp.int32, (c, LANES), 1) // RET_QK
    rows = lambda n: pl.ds(pl.multiple_of(n * c, c), c)
    heads = range(RET_PAIR)
    lg_f = [_log_sigmoid(lf_ref[j]) for j in heads]
    lg_b = [_log_sigmoid(lb_ref[j]) for j in heads]

    def per_lane(lg):
        out = lg[0]
        for j in heads[1:]:
            out = jnp.where(lane_head[:1] == j, lg[j], out)
        return out

    per_col = lambda lg: jnp.concatenate([jnp.broadcast_to(x, (1, RET_V)) for x in lg], axis=1)
    diff = row - col
    decay = [jnp.where(diff >= 0, jnp.exp(lg_f[j] * jnp.maximum(diff, 0.0)), 0.0)
             + jnp.where(diff <= 0, jnp.exp(lg_b[j] * jnp.maximum(-diff, 0.0)), 0.0) for j in heads]
    lgl_f, lgl_b = per_lane(lg_f), per_lane(lg_b)
    q_dec_f = jnp.exp(lgl_f * (row + 1.0))
    k_dec_f = jnp.exp(lgl_f * (c - 1.0 - row))
    q_dec_b = jnp.exp(lgl_b * (c - row))
    k_dec_b = jnp.exp(lgl_b * row)
    chunk_dec_f = jnp.exp(per_col(lg_f) * c)
    chunk_dec_b = jnp.exp(per_col(lg_b) * c)
    own_block = (lax.broadcasted_iota(jnp.int32, (LANES, pw), 0) // RET_QK
                 == lax.broadcasted_iota(jnp.int32, (LANES, pw), 1) // RET_V)

    def kv(n, k_dec):
        kd = (k_ref[0, rows(n), :] * k_dec).astype(BF16)
        out = lax.dot_general(kd, v_ref[0, rows(n), :], TN_DIMS, preferred_element_type=F32)
        return jnp.where(own_block, out, 0.0)

    def scan(i, states):
        sf, sb = states
        nf, nb = i, n_chunks - 1 - i
        st_ref[nf, :LANES, :] = sf.astype(BF16)
        st_ref[nb, LANES:, :] = sb.astype(BF16)
        return chunk_dec_f * sf + kv(nf, k_dec_f), chunk_dec_b * sb + kv(nb, k_dec_b)

    zero = jnp.zeros((LANES, pw), F32)
    lax.fori_loop(0, n_chunks, scan, (zero, zero), unroll=32)

    def out(n, carry):
        qc = q_ref[0, rows(n), :]
        kc = k_ref[0, rows(n), :].astype(BF16)
        vc = v_ref[0, rows(n), :]
        q_heads = jnp.concatenate([jnp.where(lane_head == j, qc, 0.0) for j in heads], axis=0)
        sc = lax.dot_general(q_heads.astype(BF16), kc, NT_DIMS, preferred_element_type=F32)
        inter = _dot(jnp.concatenate([qc * q_dec_f, qc * q_dec_b], axis=1).astype(BF16), st_ref[n])
        o = []
        for j in heads:
            a = (sc[j * c:(j + 1) * c] * decay[j]).astype(BF16)
            oj = _dot(a, vc[:, j * RET_V:(j + 1) * RET_V]) + inter[:, j * RET_V:(j + 1) * RET_V]
            o.append(_rms(oj, g_ref[...]))
        o = jnp.concatenate(o, axis=1) * _silu(gate_ref[0, rows(n), :])
        o_ref[0, rows(n), :] = o.astype(BF16)
        return carry

    lax.fori_loop(0, n_chunks, out, 0, unroll=32)


def _retention(q, k, v, gate, logit_f, logit_b, g):
    b, s, _ = q.shape
    n_chunks = s // RET_CHUNK
    pw = RET_PAIR * RET_V
    lanes = lambda x: jnp.broadcast_to(x.astype(F32)[:, None, None], (RET_HEADS, 1, LANES))
    qk = pl.BlockSpec((1, s, LANES), lambda bi, p: (bi, 0, p))
    vg = pl.BlockSpec((1, s, pw), lambda bi, p: (bi, 0, p))
    logit = pl.BlockSpec((RET_PAIR, 1, LANES), lambda bi, p: (p, 0, 0))
    return pl.pallas_call(
        _retention_kernel,
        grid=(b, RET_HEADS // RET_PAIR),
        in_specs=[qk, qk, vg, vg, logit, logit, _resident((1, RET_V))],
        out_specs=vg,
        out_shape=jax.ShapeDtypeStruct((b, s, RET_HEADS * RET_V), BF16),
        scratch_shapes=[pltpu.VMEM((n_chunks, 2 * LANES, pw), BF16)],
        compiler_params=_params(2),
        name="retention",
    )(q, k, v, gate, lanes(logit_f), lanes(logit_b), g.reshape(1, -1))


GLA_GROUP = 4


def _chunk_cumsum(x, reverse):
    n, width = x.shape
    sub = SUBLANES
    per_chunk = GLA_CHUNK // sub
    g = x.reshape(n // sub, sub, width)
    pos = lax.broadcasted_iota(jnp.int32, g.shape, 1)
    step = 1
    while step < sub:
        if reverse:
            g = g + jnp.where(pos < sub - step, pltpu.roll(g, sub - step, 1), 0.0)
        else:
            g = g + jnp.where(pos >= step, pltpu.roll(g, step, 1), 0.0)
        step *= 2
    g = g.reshape(n // GLA_CHUNK, per_chunk, sub, width)
    edge = 0 if reverse else sub - 1
    totals = jnp.broadcast_to(g[:, :, edge:edge + 1, :], g.shape)
    order = range(per_chunk - 1, -1, -1) if reverse else range(per_chunk)
    out, carry = [None] * per_chunk, None
    for k in order:
        out[k] = g[:, k] if carry is None else g[:, k] + carry
        carry = totals[:, k] if carry is None else carry + totals[:, k]
    return jnp.stack(out, axis=1).reshape(n, width)


def _gla_kernel(q_ref, k_ref, v_ref, gate_ref, cumf_ref, cumb_ref, g_ref, o_ref, stf_ref, stb_ref):
    c = GLA_CHUNK
    n_chunks = q_ref.shape[1] // c
    rows = lambda n: pl.ds(pl.multiple_of(n * c, c), c)
    cumf_ref, cumb_ref = cumf_ref.at[0], cumb_ref.at[0]
    gsz = GLA_GROUP * c
    row = lax.broadcasted_iota(jnp.int32, (gsz, gsz), 0)
    col = lax.broadcasted_iota(jnp.int32, (gsz, gsz), 1)
    same_chunk = (row // c) == (col // c)
    fwd = (cumf_ref, stf_ref, c - 1, same_chunk & (row >= col))
    bwd = (cumb_ref, stb_ref, 0, same_chunk & (row <= col))

    def step(n, state, direction):
        cum_ref, st_ref, last_row, _ = direction
        st_ref[n] = state.astype(BF16)
        cum = cum_ref[rows(n), :]
        last = cum[last_row:last_row + 1, :]
        kd = (k_ref[0, rows(n), :] * jnp.exp(last - cum)).astype(BF16)
        kv = lax.dot_general(v_ref[0, rows(n), :], kd, TN_DIMS, preferred_element_type=F32)
        return jnp.exp(last) * state + kv

    def scan(i, states):
        return step(i, states[0], fwd), step(n_chunks - 1 - i, states[1], bwd)

    zero = jnp.zeros((GLA_V, GLA_QK), F32)
    lax.fori_loop(0, n_chunks, scan, (zero, zero), unroll=32)

    def attend(g, direction):
        cum_ref, st_ref, _, keep = direction
        grows = pl.ds(pl.multiple_of(g * gsz, gsz), gsz)
        cum = cum_ref[grows, :]
        qg = (q_ref[0, grows, :] * jnp.exp(cum)).astype(BF16)
        kk = (k_ref[0, grows, :] * jnp.exp(-cum)).astype(BF16)
        sc = lax.dot_general(qg, kk, NT_DIMS, preferred_element_type=F32)
        intra = _dot(jnp.where(keep, sc, 0.0).astype(BF16), v_ref[0, grows, :])
        inter = [lax.dot_general(qg[i * c:(i + 1) * c], st_ref[g * GLA_GROUP + i], NT_DIMS,
                                 preferred_element_type=F32) for i in range(GLA_GROUP)]
        return intra + jnp.concatenate(inter, axis=0)

    def out(g, carry):
        grows = pl.ds(pl.multiple_of(g * gsz, gsz), gsz)
        o = attend(g, fwd) + attend(g, bwd)
        o_ref[0, grows, :] = (_rms(o, g_ref[...]) * _silu(gate_ref[0, grows, :])).astype(BF16)
        return carry

    lax.fori_loop(0, n_chunks // GLA_GROUP, out, 0, unroll=16)


def _gla(q, k, v, gate, cum_f, cum_b, g):
    b, s, _ = q.shape
    qk = pl.BlockSpec((1, s, GLA_QK), lambda bi, h: (bi, 0, h))
    vg = pl.BlockSpec((1, s, GLA_V), lambda bi, h: (bi, 0, h))
    return pl.pallas_call(
        _gla_kernel,
        grid=(b, GLA_HEADS),
        in_specs=[qk, qk, vg, vg, qk, qk, _resident((1, GLA_V))],
        out_specs=vg,
        out_shape=jax.ShapeDtypeStruct((b, s, GLA_HEADS * GLA_V), BF16),
        scratch_shapes=[pltpu.VMEM((s // GLA_CHUNK, GLA_V, GLA_QK), BF16),
                        pltpu.VMEM((s // GLA_CHUNK, GLA_V, GLA_QK), BF16)],
        compiler_params=_params(2),
        name="gla",
    )(q, k, v, gate, cum_f, cum_b, g.reshape(1, -1))


def kernel(x, positions, ffn1_norm, ffn1_w_gate, ffn1_w_up, ffn1_w_down, ffn2_norm, ffn2_w_gate, ffn2_w_up, ffn2_w_down, ab_norm, ab_w_in, da_lq1, da_lk1, da_lq2, da_lk2, da_norm, ret_logit_f, ret_logit_b, ret_norm, ab_w_out, c_norm, c_w_in, gla_w2_f, gla_b_f, gla_w2_b, gla_b_b, gla_norm, c_w_out, final_norm):
    b, s, d = x.shape
    t = b * s
    tables = _rope_tables(positions)
    x = x.reshape(t, d)
    seq3 = lambda a: a.reshape(b, s, -1)
    tok2 = lambda a: a.reshape(t, -1)
    ffn1 = (ffn1_norm, ffn1_w_gate.astype(BF16), ffn1_w_up.astype(BF16), ffn1_w_down.astype(BF16))
    ffn2 = (ffn2_norm, ffn2_w_gate.astype(BF16), ffn2_w_up.astype(BF16), ffn2_w_down.astype(BF16))
    for layer in range(DEPTH):
        i = layer // 2
        x = _ffn(x, layer, *ffn1, final_norm, final_norm=False)
        if layer % 2 == 0:
            lam_init = 0.8 - 0.6 * math.exp(-0.3 * layer)
            qa, ka, va, qr, kr, vr, gr = _proj_ab(x, ab_norm[i], ab_w_in[i], tables, s)
            oa = _diff_attn(seq3(qa), seq3(ka), seq3(va), da_lq1[i], da_lk1[i], da_lq2[i], da_lk2[i],
                            da_norm[i], lam_init)
            orr = _retention(seq3(qr), seq3(kr), seq3(vr), seq3(gr), ret_logit_f[i], ret_logit_b[i],
                             ret_norm[i])
            mixed, w_out = [tok2(oa), tok2(orr)], ab_w_out[i].astype(BF16)
        else:
            q, k, v, gate, cum_f, cum_b = _proj_c(x, c_norm[i], c_w_in[i], gla_w2_f[i], gla_b_f[i],
                                                  gla_w2_b[i], gla_b_b[i])
            o = _gla(seq3(q), seq3(k), seq3(v), seq3(gate), seq3(cum_f), seq3(cum_b), gla_norm[i])
            mixed, w_out = [tok2(o)], c_w_out[i].astype(BF16)
        x = _ffn(x, layer, *ffn2, final_norm, final_norm=(layer == DEPTH - 1), mixed=mixed, w_out=w_out)
    return x.reshape(b, s, d)
```

```python
import functools
import math

import jax
import jax.numpy as jnp
import numpy as np
from jax import lax
from jax.experimental import pallas as pl
from jax.experimental.pallas import tpu as pltpu

D_MODEL = 1024
D_FF = 2816
DEPTH = 2
DA_HEADS = 4
DA_DIM = 64
DA_VDIM = 2 * DA_DIM
RET_HEADS = 4
RET_QK = 64
RET_V = 128
RET_CHUNK = 128
RET_THETA = 10000.0
GLA_HEADS = 4
GLA_QK = 128
GLA_V = 256
GLA_RANK = 16
GLA_TAU = 16.0
GLA_CHUNK = 64
ROPE_THETA = 500000.0
ROPE_FRAC = 4
EPS = 1e-6

LANES = 128
SUBLANES = 8
BF16_ROWS = 2 * SUBLANES
V7X_VMEM_LIMIT_BYTES = 56 * 1024 * 1024

F32 = jnp.float32
BF16 = jnp.bfloat16
NT_DIMS = (((1,), (1,)), ((), ()))
TN_DIMS = (((0,), (0,)), ((), ()))


def _params(n_grid_dims):
    return pltpu.CompilerParams(
        dimension_semantics=("arbitrary",) * n_grid_dims,
        vmem_limit_bytes=V7X_VMEM_LIMIT_BYTES,
    )


def _resident(shape, index_map=None):
    if index_map is None:
        index_map = lambda *_: (0,) * len(shape)
    return pl.BlockSpec(shape, index_map, pipeline_mode=pl.Buffered(1))


def _rms(x, g):
    return x * lax.rsqrt(jnp.mean(x * x, axis=-1, keepdims=True) + EPS) * g


def _silu(x):
    return x * jax.nn.sigmoid(x)


def _log_sigmoid(x):
    return jnp.minimum(x, 0.0) - jnp.log(1.0 + jnp.exp(-jnp.abs(x)))


def _dot(a, b):
    return jnp.dot(a, b, preferred_element_type=F32)


def _rope_table_kernel(pos_ref, inv_a_ref, inv_r_ref, ca_ref, sa_up_ref, sa_dn_ref,
                       cr_ref, sr_up_ref, sr_dn_ref):
    pos = pos_ref[...]
    lane = lax.broadcasted_iota(jnp.int32, (pos.shape[0], LANES), 1)
    rot = DA_DIM // ROPE_FRAC
    j = lane % DA_DIM
    ang = pos * inv_a_ref[...]
    c, s = jnp.cos(ang), jnp.sin(ang)
    ca_ref[...] = jnp.where(j < rot, c, 1.0)
    sa_up_ref[...] = jnp.where(j < rot // 2, -s, 0.0)
    sa_dn_ref[...] = jnp.where((j >= rot // 2) & (j < rot), s, 0.0)
    j = lane % RET_QK
    ang = pos * inv_r_ref[...]
    c, s = jnp.cos(ang), jnp.sin(ang)
    cr_ref[...] = c
    sr_up_ref[...] = jnp.where(j < RET_QK // 2, -s, 0.0)
    sr_dn_ref[...] = jnp.where(j >= RET_QK // 2, s, 0.0)


def _rope_tables(positions):
    seq = positions.shape[0]
    ts = 512
    rot = DA_DIM // ROPE_FRAC
    inv_a = 1.0 / (ROPE_THETA ** (jnp.arange(0, rot, 2, dtype=F32) / rot))
    inv_r = 1.0 / (RET_THETA ** (jnp.arange(0, RET_QK, 2, dtype=F32) / RET_QK))
    lane = np.arange(LANES)
    inv_a_lanes = inv_a[(lane % DA_DIM) % (rot // 2)][None, :]
    inv_r_lanes = inv_r[(lane % RET_QK) % (RET_QK // 2)][None, :]
    pos = positions.astype(F32).reshape(seq, 1)
    row = pl.BlockSpec((ts, LANES), lambda i: (i, 0))
    vec = pl.BlockSpec((1, LANES), lambda i: (0, 0))
    return pl.pallas_call(
        _rope_table_kernel,
        grid=(seq // ts,),
        in_specs=[pl.BlockSpec((ts, 1), lambda i: (i, 0)), vec, vec],
        out_specs=[row] * 6,
        out_shape=[jax.ShapeDtypeStruct((seq, LANES), F32)] * 6,
        compiler_params=_params(1),
        name="rope_tables",
    )(pos, inv_a_lanes, inv_r_lanes)


def _rope(x, c, s_up, s_dn, half):
    out = []
    for b in range(x.shape[1] // LANES):
        xb = x[:, b * LANES:(b + 1) * LANES]
        up = pltpu.roll(xb, LANES - half, 1)
        dn = pltpu.roll(xb, half, 1)
        out.append(xb * c + up * s_up + dn * s_dn)
    return jnp.concatenate(out, axis=1)


def _ffn_kernel(*refs, n_mixed, n_casts, final_norm):
    n_in = 1 + 2 * n_mixed + 5 + n_casts
    x_ref, mixed = refs[0], refs[1:1 + 2 * n_mixed]
    g_ref, wg_ref, wu_ref, wd_ref, fg_ref = refs[1 + 2 * n_mixed:n_in - n_casts]
    o_ref = refs[n_in]
    for src_ref, dst_ref in zip(refs[n_in - n_casts:n_in], refs[n_in + 1:]):
        dst_ref[...] = src_ref[...].astype(BF16)
    x = x_ref[...]
    for a_ref, w_ref in zip(mixed[::2], mixed[1::2]):
        x = x + _dot(a_ref[...], w_ref[...])
    xn = _rms(x, g_ref[...]).astype(BF16)
    gate = _dot(xn, wg_ref[...])
    up = _dot(xn, wu_ref[...])
    h = (_silu(gate) * up).astype(BF16)
    out = x + 0.5 * _dot(h, wd_ref[...])
    if final_norm:
        out = _rms(out, fg_ref[...])
    o_ref[...] = out


def _ffn(x, g, wg, wu, wd, fg, *, final_norm, mixed=(), w_out=None, casts=()):
    t, d = x.shape
    tm = 512
    steps = t // tm
    tok = lambda wd_: pl.BlockSpec((tm, wd_), lambda i: (i, 0))
    in_specs, args, row = [tok(d)], [x], 0
    for a in mixed:
        wa = a.shape[1]
        in_specs += [tok(wa), _resident((wa, d), functools.partial(lambda r, i: (r, 0), row // wa))]
        args += [a, w_out]
        row += wa
    in_specs += [_resident((1, d)), _resident((d, D_FF)), _resident((d, D_FF)), _resident((D_FF, d)),
                 _resident((1, d))]
    args += [g.reshape(1, d), wg, wu, wd, fg.reshape(1, d)]
    out_specs, out_shape = [tok(d)], [jax.ShapeDtypeStruct((t, d), F32)]
    for w, layer in casts:
        _, rows, cols = w.shape
        cols = cols // LANES * LANES
        n_slabs = max(n for n in range(1, steps + 1) if rows % (n * BF16_ROWS) == 0)
        slab_of = functools.partial(lambda n, i: jnp.minimum(i, n - 1), n_slabs)
        in_specs.append(pl.BlockSpec((None, rows // n_slabs, cols),
                                     functools.partial(lambda l, f, i: (l, f(i), 0), layer, slab_of)))
        args.append(w)
        out_specs.append(pl.BlockSpec((rows // n_slabs, cols),
                                      functools.partial(lambda f, i: (f(i), 0), slab_of)))
        out_shape.append(jax.ShapeDtypeStruct((rows, cols), BF16))
    return pl.pallas_call(
        functools.partial(_ffn_kernel, n_mixed=len(mixed), n_casts=len(casts), final_norm=final_norm),
        grid=(steps,),
        in_specs=in_specs,
        out_specs=out_specs,
        out_shape=out_shape,
        compiler_params=_params(1),
        name="ffn",
    )(*args)


def _proj_ab_kernel(x_ref, g_ref, w_ref, ca_ref, sa_up_ref, sa_dn_ref, cr_ref, sr_up_ref, sr_dn_ref,
                    qa_ref, ka_ref, va_ref, qr_ref, kr_ref, vr_ref, gr_ref):
    xn = _rms(x_ref[...], g_ref[...]).astype(BF16)
    y = _dot(xn, w_ref[...])
    qa_w = DA_HEADS * 2 * DA_DIM
    va_w = DA_HEADS * DA_VDIM
    qr_w = RET_HEADS * RET_QK
    vr_w = RET_HEADS * RET_V
    o = 0
    qa, o = y[:, o:o + qa_w], o + qa_w
    ka, o = y[:, o:o + qa_w], o + qa_w
    va, o = y[:, o:o + va_w], o + va_w
    qr, o = y[:, o:o + qr_w], o + qr_w
    kr, o = y[:, o:o + qr_w], o + qr_w
    vr, o = y[:, o:o + vr_w], o + vr_w
    gr = y[:, o:o + vr_w]
    rope_a = functools.partial(_rope, c=ca_ref[...], s_up=sa_up_ref[...], s_dn=sa_dn_ref[...],
                               half=DA_DIM // ROPE_FRAC // 2)
    rope_r = functools.partial(_rope, c=cr_ref[...], s_up=sr_up_ref[...], s_dn=sr_dn_ref[...],
                               half=RET_QK // 2)
    qa_ref[...] = (rope_a(qa) * (DA_DIM ** -0.5 * math.log2(math.e))).astype(BF16)
    ka_ref[...] = rope_a(ka).astype(BF16)
    va_ref[...] = va.astype(BF16)
    qr_ref[...] = rope_r(qr)
    kr_ref[...] = rope_r(kr) * RET_QK ** -0.5
    vr_ref[...] = vr.astype(BF16)
    gr_ref[...] = gr


def _proj_ab(x, g, w, tables, seq):
    t, d = x.shape
    tm = 1024
    n_in = w.shape[1]
    widths = (DA_HEADS * 2 * DA_DIM, DA_HEADS * 2 * DA_DIM, DA_HEADS * DA_VDIM,
              RET_HEADS * RET_QK, RET_HEADS * RET_QK, RET_HEADS * RET_V, RET_HEADS * RET_V)
    dtypes = (BF16, BF16, BF16, F32, F32, BF16, F32)
    tok = lambda wd: pl.BlockSpec((tm, wd), lambda i: (i, 0))
    tab = pl.BlockSpec((tm, LANES), lambda i: (i % (seq // tm), 0))
    return pl.pallas_call(
        _proj_ab_kernel,
        grid=(t // tm,),
        in_specs=[tok(d), _resident((1, d)), _resident((d, n_in))] + [tab] * 6,
        out_specs=[tok(wd) for wd in widths],
        out_shape=[jax.ShapeDtypeStruct((t, wd), dt) for wd, dt in zip(widths, dtypes)],
        compiler_params=_params(1),
        name="proj_ab",
    )(x, g.reshape(1, d), w, *tables)


PROJ_SLAB = 256


def _proj_c_kernel(x_ref, g_ref, w_ref, wlr_ref, w2f_ref, w2b_ref, bf_ref, bb_ref,
                   q_ref, k_ref, v_ref, gate_ref, cumf_ref, cumb_ref):
    xn = _rms(x_ref[...], g_ref[...]).astype(BF16)
    lr = _dot(xn, wlr_ref[...]).astype(BF16)
    qk_w = GLA_HEADS * GLA_QK
    v_w = GLA_HEADS * GLA_V
    outs = ((q_ref, 0, lambda y: y * GLA_QK ** -0.5), (k_ref, qk_w, lambda y: y),
            (v_ref, 2 * qk_w, lambda y: y.astype(BF16)), (gate_ref, 2 * qk_w + v_w, lambda y: y))
    tm, n_main = x_ref.shape[0], w_ref.shape[1]

    def slab(lo):
        hi = lo + PROJ_SLAB
        y = _dot(xn, w_ref[:, lo:hi])
        for ref, first, post in outs:
            a, b = max(lo, first), min(hi, first + ref.shape[1])
            if a < b:
                ref[:, a - first:b - first] = post(y[:, a - lo:b - lo])

    def gates(h, direction, rows):
        w2_ref, b_ref, cum_ref, reverse = direction
        cols = slice(h * GLA_QK, (h + 1) * GLA_QK)
        log_a = _log_sigmoid(_dot(lr[rows], w2_ref[:, cols]) + b_ref[:, cols]) / GLA_TAU
        cum_ref[rows, cols] = _chunk_cumsum(log_a, reverse)

    directions = ((w2f_ref, bf_ref, cumf_ref, False), (w2b_ref, bb_ref, cumb_ref, True))
    row_blocks = [slice(r, r + tm // 2) for r in (0, tm // 2)]
    mxu_work = [functools.partial(slab, lo) for lo in range(0, n_main, PROJ_SLAB)]
    vpu_work = [functools.partial(gates, h, d, rows) for h in range(GLA_HEADS) for d in directions
                for rows in row_blocks]
    order = sorted([(i / len(mxu_work), 0, f) for i, f in enumerate(mxu_work)]
                   + [(i / len(vpu_work), 1, f) for i, f in enumerate(vpu_work)], key=lambda e: e[:2])
    for _, _, work in order:
        work()


def _proj_c(x, g, w_main, w_rank, w2_f, b_f, w2_b, b_b):
    t, d = x.shape
    tm = 512
    qk_w = GLA_HEADS * GLA_QK
    v_w = GLA_HEADS * GLA_V
    n_main = w_main.shape[1]
    w_lr = jnp.pad(w_rank.astype(BF16), ((0, 0), (0, LANES - 2 * GLA_RANK)))
    w2f = jnp.zeros((LANES, qk_w), BF16).at[:GLA_RANK].set(w2_f.astype(BF16))
    w2b = jnp.zeros((LANES, qk_w), BF16).at[GLA_RANK:2 * GLA_RANK].set(w2_b.astype(BF16))
    widths = (qk_w, qk_w, v_w, v_w, qk_w, qk_w)
    dtypes = (F32, F32, BF16, F32, F32, F32)
    tok = lambda wd: pl.BlockSpec((tm, wd), lambda i: (i, 0))
    return pl.pallas_call(
        _proj_c_kernel,
        grid=(t // tm,),
        in_specs=[tok(d), _resident((1, d)), _resident((d, n_main)), _resident((d, LANES)),
                  _resident((LANES, qk_w)), _resident((LANES, qk_w)), _resident((1, qk_w)),
                  _resident((1, qk_w))],
        out_specs=[tok(wd) for wd in widths],
        out_shape=[jax.ShapeDtypeStruct((t, wd), dt) for wd, dt in zip(widths, dtypes)],
        compiler_params=_params(1),
        name="proj_c",
    )(x, g.reshape(1, d), w_main, w_lr, w2f, w2b, b_f.reshape(1, -1), b_b.reshape(1, -1))


DA_KEY_TILE = 256


def _diff_attn_kernel(q_ref, k_ref, v_ref, lq1_ref, lk1_ref, lq2_ref, lk2_ref, g_ref, o_ref, *, lam_init):
    q = q_ref[0]
    lane = lax.broadcasted_iota(jnp.int32, q.shape, 1)
    zero = jnp.zeros_like(q)
    q_comp = (jnp.where(lane < DA_DIM, q, zero), jnp.where(lane >= DA_DIM, q, zero))
    scores = [lax.dot_general(qc, k_ref[0], NT_DIMS, preferred_element_type=F32) for qc in q_comp]
    row_max = [jnp.max(s, axis=-1, keepdims=True) for s in scores]
    tq, seq = scores[0].shape
    ones = jnp.ones((DA_KEY_TILE, LANES), BF16)
    pv = [jnp.zeros((tq, DA_VDIM + LANES), F32) for _ in scores]
    for j in range(seq // DA_KEY_TILE):
        cols = slice(j * DA_KEY_TILE, (j + 1) * DA_KEY_TILE)
        v_ones = jnp.concatenate([v_ref[0, cols, :], ones], axis=1)
        for c, s in enumerate(scores):
            pj = jnp.exp2(s[:, cols] - row_max[c]).astype(BF16)
            pv[c] = pv[c] + _dot(pj, v_ones)
    lam = (jnp.exp(jnp.sum(lq1_ref[...] * lk1_ref[...], axis=-1, keepdims=True))
           - jnp.exp(jnp.sum(lq2_ref[...] * lk2_ref[...], axis=-1, keepdims=True)) + lam_init)
    soft = [x[:, :DA_VDIM] / x[:, DA_VDIM:] for x in pv]
    o = soft[0] - lam * soft[1]
    o_ref[0] = (_rms(o, g_ref[...]) * (1.0 - lam_init)).astype(BF16)


def _diff_attn(q, k, v, lq1, lk1, lq2, lk2, g, lam_init):
    b, s, _ = q.shape
    tq = 1024
    hw = 2 * DA_DIM
    vec = lambda n: _resident((1, n))
    return pl.pallas_call(
        functools.partial(_diff_attn_kernel, lam_init=lam_init),
        grid=(b, DA_HEADS, s // tq),
        in_specs=[pl.BlockSpec((1, tq, hw), lambda bi, h, qi: (bi, qi, h)),
                  pl.BlockSpec((1, s, hw), lambda bi, h, qi: (bi, 0, h)),
                  pl.BlockSpec((1, s, DA_VDIM), lambda bi, h, qi: (bi, 0, h)),
                  vec(DA_DIM), vec(DA_DIM), vec(DA_DIM), vec(DA_DIM), vec(DA_VDIM)],
        out_specs=pl.BlockSpec((1, tq, DA_VDIM), lambda bi, h, qi: (bi, qi, h)),
        out_shape=jax.ShapeDtypeStruct((b, s, DA_HEADS * DA_VDIM), BF16),
        compiler_params=_params(3),
        name="diff_attn",
    )(q, k, v, lq1.reshape(1, -1), lk1.reshape(1, -1), lq2.reshape(1, -1), lk2.reshape(1, -1),
      g.reshape(1, -1))


RET_PAIR = LANES // RET_QK


def _retention_kernel(q_ref, k_ref, v_ref, gate_ref, lf_ref, lb_ref, g_ref, o_ref, st_ref):
    c = RET_CHUNK
    n_chunks = q_ref.shape[1] // c
    pw = RET_PAIR * RET_V
    row = lax.broadcasted_iota(jnp.int32, (c, c), 0).astype(F32)
    col = lax.broadcasted_iota(jnp.int32, (c, c), 1).astype(F32)
    lane_head = lax.broadcasted_iota(jnp.int32, (c, LANES), 1) // RET_QK
    rows = lambda n: pl.ds(pl.multiple_of(n * c, c), c)
    heads = range(RET_PAIR)
    lg_f = [_log_sigmoid(lf_ref[j]) for j in heads]
    lg_b = [_log_sigmoid(lb_ref[j]) for j in heads]

    def per_lane(lg):
        out = lg[0]
        for j in heads[1:]:
            out = jnp.where(lane_head[:1] == j, lg[j], out)
        return out

    per_col = lambda lg: jnp.concatenate([jnp.broadcast_to(x, (1, RET_V)) for x in lg], axis=1)
    diff = row - col
    decay = [jnp.where(diff >= 0, jnp.exp(lg_f[j] * jnp.maximum(diff, 0.0)), 0.0)
             + jnp.where(diff <= 0, jnp.exp(lg_b[j] * jnp.maximum(-diff, 0.0)), 0.0) for j in heads]
    lgl_f, lgl_b = per_lane(lg_f), per_lane(lg_b)
    q_dec_f = jnp.exp(lgl_f * (row + 1.0))
    k_dec_f = jnp.exp(lgl_f * (c - 1.0 - row))
    q_dec_b = jnp.exp(lgl_b * (c - row))
    k_dec_b = jnp.exp(lgl_b * row)
    chunk_dec_f = jnp.exp(per_col(lg_f) * c)
    chunk_dec_b = jnp.exp(per_col(lg_b) * c)
    own_block = (lax.broadcasted_iota(jnp.int32, (LANES, pw), 0) // RET_QK
                 == lax.broadcasted_iota(jnp.int32, (LANES, pw), 1) // RET_V)

    def kv(n, k_dec):
        kd = (k_ref[0, rows(n), :] * k_dec).astype(BF16)
        out = lax.dot_general(kd, v_ref[0, rows(n), :], TN_DIMS, preferred_element_type=F32)
        return jnp.where(own_block, out, 0.0)

    def scan(i, states):
        sf, sb = states
        nf, nb = i, n_chunks - 1 - i
        st_ref[nf, :LANES, :] = sf.astype(BF16)
        st_ref[nb, LANES:, :] = sb.astype(BF16)
        return chunk_dec_f * sf + kv(nf, k_dec_f), chunk_dec_b * sb + kv(nb, k_dec_b)

    zero = jnp.zeros((LANES, pw), F32)
    lax.fori_loop(0, n_chunks, scan, (zero, zero), unroll=32)

    def out(n, carry):
        qc = q_ref[0, rows(n), :]
        kc = k_ref[0, rows(n), :].astype(BF16)
        vc = v_ref[0, rows(n), :]
        q_heads = jnp.concatenate([jnp.where(lane_head == j, qc, 0.0) for j in heads], axis=0)
        sc = lax.dot_general(q_heads.astype(BF16), kc, NT_DIMS, preferred_element_type=F32)
        inter = _dot(jnp.concatenate([qc * q_dec_f, qc * q_dec_b], axis=1).astype(BF16), st_ref[n])
        o = []
        for j in heads:
            a = (sc[j * c:(j + 1) * c] * decay[j]).astype(BF16)
            oj = _dot(a, vc[:, j * RET_V:(j + 1) * RET_V]) + inter[:, j * RET_V:(j + 1) * RET_V]
            o.append(_rms(oj, g_ref[...]))
        o = jnp.concatenate(o, axis=1) * _silu(gate_ref[0, rows(n), :])
        o_ref[0, rows(n), :] = o.astype(BF16)
        return carry

    lax.fori_loop(0, n_chunks, out, 0, unroll=32)


def _retention(q, k, v, gate, logit_f, logit_b, g):
    b, s, _ = q.shape
    n_chunks = s // RET_CHUNK
    pw = RET_PAIR * RET_V
    lanes = lambda x: jnp.broadcast_to(x.astype(F32)[:, None, None], (RET_HEADS, 1, LANES))
    qk = pl.BlockSpec((1, s, LANES), lambda bi, p: (bi, 0, p))
    vg = pl.BlockSpec((1, s, pw), lambda bi, p: (bi, 0, p))
    logit = pl.BlockSpec((RET_PAIR, 1, LANES), lambda bi, p: (p, 0, 0))
    return pl.pallas_call(
        _retention_kernel,
        grid=(b, RET_HEADS // RET_PAIR),
        in_specs=[qk, qk, vg, vg, logit, logit, _resident((1, RET_V))],
        out_specs=vg,
        out_shape=jax.ShapeDtypeStruct((b, s, RET_HEADS * RET_V), BF16),
        scratch_shapes=[pltpu.VMEM((n_chunks, 2 * LANES, pw), BF16)],
        compiler_params=_params(2),
        name="retention",
    )(q, k, v, gate, lanes(logit_f), lanes(logit_b), g.reshape(1, -1))


GLA_GROUP = 4


def _chunk_cumsum(x, reverse):
    n, width = x.shape
    sub = SUBLANES
    per_chunk = GLA_CHUNK // sub
    g = x.reshape(n // sub, sub, width)
    pos = lax.broadcasted_iota(jnp.int32, g.shape, 1)
    step = 1
    while step < sub:
        if reverse:
            g = g + jnp.where(pos < sub - step, pltpu.roll(g, sub - step, 1), 0.0)
        else:
            g = g + jnp.where(pos >= step, pltpu.roll(g, step, 1), 0.0)
        step *= 2
    g = g.reshape(n // GLA_CHUNK, per_chunk, sub, width)
    edge = 0 if reverse else sub - 1
    totals = jnp.broadcast_to(g[:, :, edge:edge + 1, :], g.shape)
    order = range(per_chunk - 1, -1, -1) if reverse else range(per_chunk)
    out, carry = [None] * per_chunk, None
    for k in order:
        out[k] = g[:, k] if carry is None else g[:, k] + carry
        carry = totals[:, k] if carry is None else carry + totals[:, k]
    return jnp.stack(out, axis=1).reshape(n, width)


def _gla_kernel(q_ref, k_ref, v_ref, gate_ref, cumf_ref, cumb_ref, g_ref, o_ref, stf_ref, stb_ref):
    c = GLA_CHUNK
    n_chunks = q_ref.shape[1] // c
    rows = lambda n: pl.ds(pl.multiple_of(n * c, c), c)
    cumf_ref, cumb_ref = cumf_ref.at[0], cumb_ref.at[0]
    gsz = GLA_GROUP * c
    row = lax.broadcasted_iota(jnp.int32, (gsz, gsz), 0)
    col = lax.broadcasted_iota(jnp.int32, (gsz, gsz), 1)
    same_chunk = (row // c) == (col // c)
    fwd = (cumf_ref, stf_ref, c - 1, same_chunk & (row >= col))
    bwd = (cumb_ref, stb_ref, 0, same_chunk & (row <= col))

    def step(n, state, direction):
        cum_ref, st_ref, last_row, _ = direction
        st_ref[n] = state.astype(BF16)
        cum = cum_ref[rows(n), :]
        last = cum[last_row:last_row + 1, :]
        kd = (k_ref[0, rows(n), :] * jnp.exp(last - cum)).astype(BF16)
        kv = lax.dot_general(v_ref[0, rows(n), :], kd, TN_DIMS, preferred_element_type=F32)
        return jnp.exp(last) * state + kv

    def scan(i, states):
        return step(i, states[0], fwd), step(n_chunks - 1 - i, states[1], bwd)

    zero = jnp.zeros((GLA_V, GLA_QK), F32)
    lax.fori_loop(0, n_chunks, scan, (zero, zero), unroll=32)

    def attend(g, direction):
        cum_ref, st_ref, _, keep = direction
        grows = pl.ds(pl.multiple_of(g * gsz, gsz), gsz)
        cum = cum_ref[grows, :]
        qg = (q_ref[0, grows, :] * jnp.exp(cum)).astype(BF16)
        kk = (k_ref[0, grows, :] * jnp.exp(-cum)).astype(BF16)
        sc = lax.dot_general(qg, kk, NT_DIMS, preferred_element_type=F32)
        intra = _dot(jnp.where(keep, sc, 0.0).astype(BF16), v_ref[0, grows, :])
        inter = [lax.dot_general(qg[i * c:(i + 1) * c], st_ref[g * GLA_GROUP + i], NT_DIMS,
                                 preferred_element_type=F32) for i in range(GLA_GROUP)]
        return intra + jnp.concatenate(inter, axis=0)

    def out(g, carry):
        grows = pl.ds(pl.multiple_of(g * gsz, gsz), gsz)
        o = attend(g, fwd) + attend(g, bwd)
        o_ref[0, grows, :] = (_rms(o, g_ref[...]) * _silu(gate_ref[0, grows, :])).astype(BF16)
        return carry

    lax.fori_loop(0, n_chunks // GLA_GROUP, out, 0, unroll=16)


def _gla(q, k, v, gate, cum_f, cum_b, g):
    b, s, _ = q.shape
    qk = pl.BlockSpec((1, s, GLA_QK), lambda bi, h: (bi, 0, h))
    vg = pl.BlockSpec((1, s, GLA_V), lambda bi, h: (bi, 0, h))
    return pl.pallas_call(
        _gla_kernel,
        grid=(b, GLA_HEADS),
        in_specs=[qk, qk, vg, vg, qk, qk, _resident((1, GLA_V))],
        out_specs=vg,
        out_shape=jax.ShapeDtypeStruct((b, s, GLA_HEADS * GLA_V), BF16),
        scratch_shapes=[pltpu.VMEM((s // GLA_CHUNK, GLA_V, GLA_QK), BF16),
                        pltpu.VMEM((s // GLA_CHUNK, GLA_V, GLA_QK), BF16)],
        compiler_params=_params(2),
        name="gla",
    )(q, k, v, gate, cum_f, cum_b, g.reshape(1, -1))


def kernel(x, positions, ffn1_norm, ffn1_w_gate, ffn1_w_up, ffn1_w_down, ffn2_norm, ffn2_w_gate, ffn2_w_up, ffn2_w_down, ab_norm, ab_w_in, da_lq1, da_lk1, da_lq2, da_lk2, da_norm, ret_logit_f, ret_logit_b, ret_norm, ab_w_out, c_norm, c_w_in, gla_w2_f, gla_b_f, gla_w2_b, gla_b_b, gla_norm, c_w_out, final_norm):
    b, s, d = x.shape
    t = b * s
    tables = _rope_tables(positions)
    x = x.reshape(t, d)
    seq3 = lambda a: a.reshape(b, s, -1)
    tok2 = lambda a: a.reshape(t, -1)
    ffn_casts = lambda wg, wu, wd, layer: [(wg, layer), (wu, layer), (wd, layer)]
    ffn1 = (ffn1_w_gate[0].astype(BF16), ffn1_w_up[0].astype(BF16), ffn1_w_down[0].astype(BF16))
    for layer in range(DEPTH):
        i = layer // 2
        w_in, w_out = (ab_w_in, ab_w_out) if layer % 2 == 0 else (c_w_in, c_w_out)
        casts = ffn_casts(ffn2_w_gate, ffn2_w_up, ffn2_w_down, layer) + [(w_in, i), (w_out, i)]
        x, *cast = _ffn(x, ffn1_norm[layer], *ffn1, final_norm, final_norm=False, casts=casts)
        ffn2, (w_in, w_out) = cast[:3], cast[3:]
        if layer % 2 == 0:
            lam_init = 0.8 - 0.6 * math.exp(-0.3 * layer)
            qa, ka, va, qr, kr, vr, gr = _proj_ab(x, ab_norm[i], w_in, tables, s)
            oa = _diff_attn(seq3(qa), seq3(ka), seq3(va), da_lq1[i], da_lk1[i], da_lq2[i], da_lk2[i],
                            da_norm[i], lam_init)
            orr = _retention(seq3(qr), seq3(kr), seq3(vr), seq3(gr), ret_logit_f[i], ret_logit_b[i],
                             ret_norm[i])
            mixed = [tok2(oa), tok2(orr)]
        else:
            q, k, v, gate, cum_f, cum_b = _proj_c(x, c_norm[i], w_in, c_w_in[i][:, w_in.shape[1]:],
                                                  gla_w2_f[i], gla_b_f[i], gla_w2_b[i], gla_b_b[i])
            o = _gla(seq3(q), seq3(k), seq3(v), seq3(gate), seq3(cum_f), seq3(cum_b), gla_norm[i])
            mixed = [tok2(o)]
        casts = ffn_casts(ffn1_w_gate, ffn1_w_up, ffn1_w_down, layer + 1) if layer + 1 < DEPTH else []
        x, *cast = _ffn(x, ffn2_norm[layer], *ffn2, final_norm, final_norm=(layer == DEPTH - 1),
                        mixed=mixed, w_out=w_out, casts=casts)
        if cast:
            ffn1 = cast
    return x.reshape(b, s, d)
```

```python
import functools
import math

import jax
import jax.numpy as jnp
import numpy as np
from jax import lax
from jax.experimental import pallas as pl
from jax.experimental.pallas import tpu as pltpu

D_FF = 2816
DEPTH = 2
DA_HEADS = 4
DA_DIM = 64
DA_VDIM = 2 * DA_DIM
RET_HEADS = 4
RET_QK = 64
RET_V = 128
RET_CHUNK = 128
RET_THETA = 10000.0
GLA_HEADS = 4
GLA_QK = 128
GLA_V = 256
GLA_RANK = 16
GLA_TAU = 16.0
GLA_CHUNK = 64
ROPE_THETA = 500000.0
ROPE_FRAC = 4
EPS = 1e-6

LANES = 128
SUBLANES = 8
BF16_ROWS = 2 * SUBLANES
V7X_VMEM_LIMIT_BYTES = 56 * 1024 * 1024

ROPE_ROW_TILE = 512
FFN_TOKEN_TILE = 512
PROJ_AB_TOKEN_TILE = 1024
PROJ_C_TOKEN_TILE = 512
DA_QUERY_TILE = 1024

F32 = jnp.float32
BF16 = jnp.bfloat16
NT_DIMS = (((1,), (1,)), ((), ()))
TN_DIMS = (((0,), (0,)), ((), ()))


def _params(n_grid_dims):
    return pltpu.CompilerParams(
        dimension_semantics=("arbitrary",) * n_grid_dims,
        vmem_limit_bytes=V7X_VMEM_LIMIT_BYTES,
    )


def _resident(shape, index_map=None):
    if index_map is None:
        index_map = lambda *_: (0,) * len(shape)
    return pl.BlockSpec(shape, index_map, pipeline_mode=pl.Buffered(1))


def _rms(x, g):
    return x * lax.rsqrt(jnp.mean(x * x, axis=-1, keepdims=True) + EPS) * g


def _silu(x):
    return x * jax.nn.sigmoid(x)


def _log_sigmoid(x):
    return jnp.minimum(x, 0.0) - jnp.log(1.0 + jnp.exp(-jnp.abs(x)))


def _dot(a, b):
    return jnp.dot(a, b, preferred_element_type=F32)


def _rope_table_kernel(pos_ref, inv_a_ref, inv_r_ref, ca_ref, sa_up_ref, sa_dn_ref,
                       cr_ref, sr_up_ref, sr_dn_ref):
    pos = pos_ref[...]
    lane = lax.broadcasted_iota(jnp.int32, (pos.shape[0], LANES), 1)
    rot = DA_DIM // ROPE_FRAC
    j = lane % DA_DIM
    ang = pos * inv_a_ref[...]
    c, s = jnp.cos(ang), jnp.sin(ang)
    ca_ref[...] = jnp.where(j < rot, c, 1.0)
    sa_up_ref[...] = jnp.where(j < rot // 2, -s, 0.0)
    sa_dn_ref[...] = jnp.where((j >= rot // 2) & (j < rot), s, 0.0)
    j = lane % RET_QK
    ang = pos * inv_r_ref[...]
    c, s = jnp.cos(ang), jnp.sin(ang)
    cr_ref[...] = c
    sr_up_ref[...] = jnp.where(j < RET_QK // 2, -s, 0.0)
    sr_dn_ref[...] = jnp.where(j >= RET_QK // 2, s, 0.0)


def _rope_tables(positions):
    seq = positions.shape[0]
    ts = ROPE_ROW_TILE
    rot = DA_DIM // ROPE_FRAC
    inv_a = 1.0 / (ROPE_THETA ** (jnp.arange(0, rot, 2, dtype=F32) / rot))
    inv_r = 1.0 / (RET_THETA ** (jnp.arange(0, RET_QK, 2, dtype=F32) / RET_QK))
    lane = np.arange(LANES)
    inv_a_lanes = inv_a[(lane % DA_DIM) % (rot // 2)][None, :]
    inv_r_lanes = inv_r[(lane % RET_QK) % (RET_QK // 2)][None, :]
    pos = positions.astype(F32).reshape(seq, 1)
    row = pl.BlockSpec((ts, LANES), lambda i: (i, 0))
    vec = pl.BlockSpec((1, LANES), lambda i: (0, 0))
    return pl.pallas_call(
        _rope_table_kernel,
        grid=(seq // ts,),
        in_specs=[pl.BlockSpec((ts, 1), lambda i: (i, 0)), vec, vec],
        out_specs=[row] * 6,
        out_shape=[jax.ShapeDtypeStruct((seq, LANES), F32)] * 6,
        compiler_params=_params(1),
        name="rope_tables",
    )(pos, inv_a_lanes, inv_r_lanes)


def _rope(x, c, s_up, s_dn, half):
    out = []
    for b in range(x.shape[1] // LANES):
        xb = x[:, b * LANES:(b + 1) * LANES]
        up = pltpu.roll(xb, LANES - half, 1)
        dn = pltpu.roll(xb, half, 1)
        out.append(xb * c + up * s_up + dn * s_dn)
    return jnp.concatenate(out, axis=1)


def _ffn_kernel(*refs, n_mixed, cast_tails, final_norm):
    n_casts = len(cast_tails)
    n_in = 1 + 2 * n_mixed + 5 + n_casts
    x_ref, mixed = refs[0], refs[1:1 + 2 * n_mixed]
    g_ref, wg_ref, wu_ref, wd_ref, fg_ref = refs[1 + 2 * n_mixed:n_in - n_casts]
    o_ref = refs[n_in]
    for src_ref, dst_ref, tail in zip(refs[n_in - n_casts:n_in], refs[n_in + 1:], cast_tails):
        w = src_ref[...].astype(BF16)
        if tail is not None:
            w = jnp.concatenate([w[:, tail:], jnp.zeros((w.shape[0], LANES - (w.shape[1] - tail)), BF16)],
                                axis=1)
        dst_ref[...] = w
    x = x_ref[...]
    for a_ref, w_ref in zip(mixed[::2], mixed[1::2]):
        x = x + _dot(a_ref[...], w_ref[...])
    xn = _rms(x, g_ref[...]).astype(BF16)
    gate = _dot(xn, wg_ref[...])
    up = _dot(xn, wu_ref[...])
    h = (_silu(gate) * up).astype(BF16)
    out = x + 0.5 * _dot(h, wd_ref[...])
    if final_norm:
        out = _rms(out, fg_ref[...])
    o_ref[...] = out


def _ffn(x, g, wg, wu, wd, fg, *, final_norm, mixed=(), w_out=None, casts=()):
    t, d = x.shape
    tm = FFN_TOKEN_TILE
    steps = t // tm
    tok = lambda wd_: pl.BlockSpec((tm, wd_), lambda i: (i, 0))
    in_specs, args, row = [tok(d)], [x], 0
    for a in mixed:
        wa = a.shape[1]
        in_specs += [tok(wa), _resident((wa, d), functools.partial(lambda r, i: (r, 0), row // wa))]
        args += [a, w_out]
        row += wa
    in_specs += [_resident((1, d)), _resident((d, D_FF)), _resident((d, D_FF)), _resident((D_FF, d)),
                 _resident((1, d))]
    args += [g.reshape(1, d), wg, wu, wd, fg.reshape(1, d)]
    out_specs, out_shape = [tok(d)], [jax.ShapeDtypeStruct((t, d), F32)]
    for w, layer, tail in casts:
        _, rows, cols = w.shape
        cols_in, cols_out = (cols, LANES) if tail is not None else (cols // LANES * LANES,) * 2
        n_slabs = max(n for n in range(1, steps + 1) if rows % (n * BF16_ROWS) == 0)
        slab_of = functools.partial(lambda n, i: jnp.minimum(i, n - 1), n_slabs)
        in_specs.append(pl.BlockSpec((None, rows // n_slabs, cols_in),
                                     functools.partial(lambda l, f, i: (l, f(i), 0), layer, slab_of)))
        args.append(w)
        out_specs.append(pl.BlockSpec((rows // n_slabs, cols_out),
                                      functools.partial(lambda f, i: (f(i), 0), slab_of)))
        out_shape.append(jax.ShapeDtypeStruct((rows, cols_out), BF16))
    return pl.pallas_call(
        functools.partial(_ffn_kernel, n_mixed=len(mixed), cast_tails=tuple(c[2] for c in casts),
                          final_norm=final_norm),
        grid=(steps,),
        in_specs=in_specs,
        out_specs=out_specs,
        out_shape=out_shape,
        compiler_params=_params(1),
        name="ffn",
    )(*args)


def _proj_ab_kernel(x_ref, g_ref, w_ref, ca_ref, sa_up_ref, sa_dn_ref, cr_ref, sr_up_ref, sr_dn_ref,
                    qa_ref, ka_ref, va_ref, qr_ref, kr_ref, vr_ref, gr_ref):
    xn = _rms(x_ref[...], g_ref[...]).astype(BF16)
    y = _dot(xn, w_ref[...])
    qa_w = DA_HEADS * 2 * DA_DIM
    va_w = DA_HEADS * DA_VDIM
    qr_w = RET_HEADS * RET_QK
    vr_w = RET_HEADS * RET_V
    o = 0
    qa, o = y[:, o:o + qa_w], o + qa_w
    ka, o = y[:, o:o + qa_w], o + qa_w
    va, o = y[:, o:o + va_w], o + va_w
    qr, o = y[:, o:o + qr_w], o + qr_w
    kr, o = y[:, o:o + qr_w], o + qr_w
    vr, o = y[:, o:o + vr_w], o + vr_w
    gr = y[:, o:o + vr_w]
    rope_a = functools.partial(_rope, c=ca_ref[...], s_up=sa_up_ref[...], s_dn=sa_dn_ref[...],
                               half=DA_DIM // ROPE_FRAC // 2)
    rope_r = functools.partial(_rope, c=cr_ref[...], s_up=sr_up_ref[...], s_dn=sr_dn_ref[...],
                               half=RET_QK // 2)
    qa_ref[...] = (rope_a(qa) * (DA_DIM ** -0.5 * math.log2(math.e))).astype(BF16)
    ka_ref[...] = rope_a(ka).astype(BF16)
    va_ref[...] = va.astype(BF16)
    qr_ref[...] = rope_r(qr)
    kr_ref[...] = rope_r(kr) * RET_QK ** -0.5
    vr_ref[...] = vr.astype(BF16)
    gr_ref[...] = gr


def _proj_ab(x, g, w, tables, seq):
    t, d = x.shape
    tm = PROJ_AB_TOKEN_TILE
    n_in = w.shape[1]
    widths = (DA_HEADS * 2 * DA_DIM, DA_HEADS * 2 * DA_DIM, DA_HEADS * DA_VDIM,
              RET_HEADS * RET_QK, RET_HEADS * RET_QK, RET_HEADS * RET_V, RET_HEADS * RET_V)
    dtypes = (BF16, BF16, BF16, F32, F32, BF16, F32)
    tok = lambda wd: pl.BlockSpec((tm, wd), lambda i: (i, 0))
    tab = pl.BlockSpec((tm, LANES), lambda i: (i % (seq // tm), 0))
    return pl.pallas_call(
        _proj_ab_kernel,
        grid=(t // tm,),
        in_specs=[tok(d), _resident((1, d)), _resident((d, n_in))] + [tab] * 6,
        out_specs=[tok(wd) for wd in widths],
        out_shape=[jax.ShapeDtypeStruct((t, wd), dt) for wd, dt in zip(widths, dtypes)],
        compiler_params=_params(1),
        name="proj_ab",
    )(x, g.reshape(1, d), w, *tables)


PROJ_SLAB = 256


def _proj_c_kernel(x_ref, g_ref, w_ref, wlr_ref, w2f_ref, w2b_ref, bf_ref, bb_ref,
                   q_ref, k_ref, v_ref, gate_ref, cumf_ref, cumb_ref):
    xn = _rms(x_ref[...], g_ref[...]).astype(BF16)
    lr = _dot(xn, wlr_ref[...]).astype(BF16)
    qk_w = GLA_HEADS * GLA_QK
    v_w = GLA_HEADS * GLA_V
    outs = ((q_ref, 0, lambda y: y * GLA_QK ** -0.5), (k_ref, qk_w, lambda y: y),
            (v_ref, 2 * qk_w, lambda y: y.astype(BF16)), (gate_ref, 2 * qk_w + v_w, lambda y: y))
    tm, n_main = x_ref.shape[0], w_ref.shape[1]

    def slab(lo):
        hi = lo + PROJ_SLAB
        y = _dot(xn, w_ref[:, lo:hi])
        for ref, first, post in outs:
            a, b = max(lo, first), min(hi, first + ref.shape[1])
            if a < b:
                ref[:, a - first:b - first] = post(y[:, a - lo:b - lo])

    def gates(h, direction, rows):
        w2_ref, b_ref, cum_ref, reverse = direction
        cols = slice(h * GLA_QK, (h + 1) * GLA_QK)
        log_a = _log_sigmoid(_dot(lr[rows], w2_ref[:, cols]) + b_ref[:, cols]) / GLA_TAU
        cum_ref[rows, cols] = _chunk_cumsum(log_a, reverse)

    directions = ((w2f_ref, bf_ref, cumf_ref, False), (w2b_ref, bb_ref, cumb_ref, True))
    row_blocks = [slice(r, r + tm // 2) for r in (0, tm // 2)]
    mxu_work = [functools.partial(slab, lo) for lo in range(0, n_main, PROJ_SLAB)]
    vpu_work = [functools.partial(gates, h, d, rows) for h in range(GLA_HEADS) for d in directions
                for rows in row_blocks]
    order = sorted([(i / len(mxu_work), 0, f) for i, f in enumerate(mxu_work)]
                   + [(i / len(vpu_work), 1, f) for i, f in enumerate(vpu_work)], key=lambda e: e[:2])
    for _, _, work in order:
        work()


def _proj_c(x, g, w_main, w_lr, w2_f, b_f, w2_b, b_b):
    t, d = x.shape
    tm = PROJ_C_TOKEN_TILE
    qk_w = GLA_HEADS * GLA_QK
    v_w = GLA_HEADS * GLA_V
    n_main = w_main.shape[1]
    w2f = jnp.zeros((LANES, qk_w), BF16).at[:GLA_RANK].set(w2_f.astype(BF16))
    w2b = jnp.zeros((LANES, qk_w), BF16).at[GLA_RANK:2 * GLA_RANK].set(w2_b.astype(BF16))
    widths = (qk_w, qk_w, v_w, v_w, qk_w, qk_w)
    dtypes = (F32, F32, BF16, F32, F32, F32)
    tok = lambda wd: pl.BlockSpec((tm, wd), lambda i: (i, 0))
    return pl.pallas_call(
        _proj_c_kernel,
        grid=(t // tm,),
        in_specs=[tok(d), _resident((1, d)), _resident((d, n_main)), _resident((d, LANES)),
                  _resident((LANES, qk_w)), _resident((LANES, qk_w)), _resident((1, qk_w)),
                  _resident((1, qk_w))],
        out_specs=[tok(wd) for wd in widths],
        out_shape=[jax.ShapeDtypeStruct((t, wd), dt) for wd, dt in zip(widths, dtypes)],
        compiler_params=_params(1),
        name="proj_c",
    )(x, g.reshape(1, d), w_main, w_lr, w2f, w2b, b_f.reshape(1, -1), b_b.reshape(1, -1))


DA_KEY_TILE = 256


def _diff_attn_kernel(q_ref, k_ref, v_ref, lq1_ref, lk1_ref, lq2_ref, lk2_ref, g_ref, o_ref, *, lam_init):
    q = q_ref[0]
    lane = lax.broadcasted_iota(jnp.int32, q.shape, 1)
    zero = jnp.zeros_like(q)
    q_comp = (jnp.where(lane < DA_DIM, q, zero), jnp.where(lane >= DA_DIM, q, zero))
    scores = [lax.dot_general(qc, k_ref[0], NT_DIMS, preferred_element_type=F32) for qc in q_comp]
    row_max = [jnp.max(s, axis=-1, keepdims=True) for s in scores]
    tq, seq = scores[0].shape
    ones = jnp.ones((DA_KEY_TILE, LANES), BF16)
    pv = [jnp.zeros((tq, DA_VDIM + LANES), F32) for _ in scores]
    for j in range(seq // DA_KEY_TILE):
        cols = slice(j * DA_KEY_TILE, (j + 1) * DA_KEY_TILE)
        v_ones = jnp.concatenate([v_ref[0, cols, :], ones], axis=1)
        for c, s in enumerate(scores):
            pj = jnp.exp2(s[:, cols] - row_max[c]).astype(BF16)
            pv[c] = pv[c] + _dot(pj, v_ones)
    lam = (jnp.exp(jnp.sum(lq1_ref[...] * lk1_ref[...], axis=-1, keepdims=True))
           - jnp.exp(jnp.sum(lq2_ref[...] * lk2_ref[...], axis=-1, keepdims=True)) + lam_init)
    soft = [x[:, :DA_VDIM] / x[:, DA_VDIM:] for x in pv]
    o = soft[0] - lam * soft[1]
    o_ref[0] = (_rms(o, g_ref[...]) * (1.0 - lam_init)).astype(BF16)


def _diff_attn(q, k, v, lq1, lk1, lq2, lk2, g, lam_init):
    b, s, _ = q.shape
    tq = DA_QUERY_TILE
    hw = 2 * DA_DIM
    vec = lambda n: _resident((1, n))
    return pl.pallas_call(
        functools.partial(_diff_attn_kernel, lam_init=lam_init),
        grid=(b, DA_HEADS, s // tq),
        in_specs=[pl.BlockSpec((1, tq, hw), lambda bi, h, qi: (bi, qi, h)),
                  pl.BlockSpec((1, s, hw), lambda bi, h, qi: (bi, 0, h)),
                  pl.BlockSpec((1, s, DA_VDIM), lambda bi, h, qi: (bi, 0, h)),
                  vec(DA_DIM), vec(DA_DIM), vec(DA_DIM), vec(DA_DIM), vec(DA_VDIM)],
        out_specs=pl.BlockSpec((1, tq, DA_VDIM), lambda bi, h, qi: (bi, qi, h)),
        out_shape=jax.ShapeDtypeStruct((b, s, DA_HEADS * DA_VDIM), BF16),
        compiler_params=_params(3),
        name="diff_attn",
    )(q, k, v, lq1.reshape(1, -1), lk1.reshape(1, -1), lq2.reshape(1, -1), lk2.reshape(1, -1),
      g.reshape(1, -1))


RET_PAIR = LANES // RET_QK


def _retention_kernel(q_ref, k_ref, v_ref, gate_ref, lf_ref, lb_ref, g_ref, o_ref, st_ref):
    c = RET_CHUNK
    n_chunks = q_ref.shape[1] // c
    pw = RET_PAIR * RET_V
    row = lax.broadcasted_iota(jnp.int32, (c, c), 0).astype(F32)
    col = lax.broadcasted_iota(jnp.int32, (c, c), 1).astype(F32)
    lane_head = lax.broadcasted_iota(jnp.int32, (c, LANES), 1) // RET_QK
    rows = lambda n: pl.ds(pl.multiple_of(n * c, c), c)
    heads = range(RET_PAIR)
    lg_f = [_log_sigmoid(lf_ref[j]) for j in heads]
    lg_b = [_log_sigmoid(lb_ref[j]) for j in heads]

    def per_lane(lg):
        out = lg[0]
        for j in heads[1:]:
            out = jnp.where(lane_head[:1] == j, lg[j], out)
        return out

    per_col = lambda lg: jnp.concatenate([jnp.broadcast_to(x, (1, RET_V)) for x in lg], axis=1)
    diff = row - col
    decay = [jnp.where(diff >= 0, jnp.exp(lg_f[j] * jnp.maximum(diff, 0.0)), 0.0)
             + jnp.where(diff <= 0, jnp.exp(lg_b[j] * jnp.maximum(-diff, 0.0)), 0.0) for j in heads]
    lgl_f, lgl_b = per_lane(lg_f), per_lane(lg_b)
    q_dec_f = jnp.exp(lgl_f * (row + 1.0))
    k_dec_f = jnp.exp(lgl_f * (c - 1.0 - row))
    q_dec_b = jnp.exp(lgl_b * (c - row))
    k_dec_b = jnp.exp(lgl_b * row)
    chunk_dec_f = jnp.exp(per_col(lg_f) * c)
    chunk_dec_b = jnp.exp(per_col(lg_b) * c)
    own_block = (lax.broadcasted_iota(jnp.int32, (LANES, pw), 0) // RET_QK
                 == lax.broadcasted_iota(jnp.int32, (LANES, pw), 1) // RET_V)

    def kv(n, k_dec):
        kd = (k_ref[0, rows(n), :] * k_dec).astype(BF16)
        out = lax.dot_general(kd, v_ref[0, rows(n), :], TN_DIMS, preferred_element_type=F32)
        return jnp.where(own_block, out, 0.0)

    def scan(i, states):
        sf, sb = states
        nf, nb = i, n_chunks - 1 - i
        st_ref[nf, :LANES, :] = sf.astype(BF16)
        st_ref[nb, LANES:, :] = sb.astype(BF16)
        return chunk_dec_f * sf + kv(nf, k_dec_f), chunk_dec_b * sb + kv(nb, k_dec_b)

    zero = jnp.zeros((LANES, pw), F32)
    lax.fori_loop(0, n_chunks, scan, (zero, zero), unroll=32)

    def out(n, carry):
        qc = q_ref[0, rows(n), :]
        kc = k_ref[0, rows(n), :].astype(BF16)
        vc = v_ref[0, rows(n), :]
        q_heads = jnp.concatenate([jnp.where(lane_head == j, qc, 0.0) for j in heads], axis=0)
        sc = lax.dot_general(q_heads.astype(BF16), kc, NT_DIMS, preferred_element_type=F32)
        inter = _dot(jnp.concatenate([qc * q_dec_f, qc * q_dec_b], axis=1).astype(BF16), st_ref[n])
        o = []
        for j in heads:
            a = (sc[j * c:(j + 1) * c] * decay[j]).astype(BF16)
            oj = _dot(a, vc[:, j * RET_V:(j + 1) * RET_V]) + inter[:, j * RET_V:(j + 1) * RET_V]
            o.append(_rms(oj, g_ref[...]))
        o = jnp.concatenate(o, axis=1) * _silu(gate_ref[0, rows(n), :])
        o_ref[0, rows(n), :] = o.astype(BF16)
        return carry

    lax.fori_loop(0, n_chunks, out, 0, unroll=32)


def _retention(q, k, v, gate, logit_f, logit_b, g):
    b, s, _ = q.shape
    n_chunks = s // RET_CHUNK
    pw = RET_PAIR * RET_V
    lanes = lambda x: jnp.broadcast_to(x.astype(F32)[:, None, None], (RET_HEADS, 1, LANES))
    qk = pl.BlockSpec((1, s, LANES), lambda bi, p: (bi, 0, p))
    vg = pl.BlockSpec((1, s, pw), lambda bi, p: (bi, 0, p))
    logit = pl.BlockSpec((RET_PAIR, 1, LANES), lambda bi, p: (p, 0, 0))
    return pl.pallas_call(
        _retention_kernel,
        grid=(b, RET_HEADS // RET_PAIR),
        in_specs=[qk, qk, vg, vg, logit, logit, _resident((1, RET_V))],
        out_specs=vg,
        out_shape=jax.ShapeDtypeStruct((b, s, RET_HEADS * RET_V), BF16),
        scratch_shapes=[pltpu.VMEM((n_chunks, 2 * LANES, pw), BF16)],
        compiler_params=_params(2),
        name="retention",
    )(q, k, v, gate, lanes(logit_f), lanes(logit_b), g.reshape(1, -1))


GLA_GROUP = 4


def _chunk_cumsum(x, reverse):
    n, width = x.shape
    sub = SUBLANES
    per_chunk = GLA_CHUNK // sub
    g = x.reshape(n // sub, sub, width)
    pos = lax.broadcasted_iota(jnp.int32, g.shape, 1)
    step = 1
    while step < sub:
        if reverse:
            g = g + jnp.where(pos < sub - step, pltpu.roll(g, sub - step, 1), 0.0)
        else:
            g = g + jnp.where(pos >= step, pltpu.roll(g, step, 1), 0.0)
        step *= 2
    g = g.reshape(n // GLA_CHUNK, per_chunk, sub, width)
    edge = 0 if reverse else sub - 1
    totals = jnp.broadcast_to(g[:, :, edge:edge + 1, :], g.shape)
    order = range(per_chunk - 1, -1, -1) if reverse else range(per_chunk)
    out, carry = [None] * per_chunk, None
    for k in order:
        out[k] = g[:, k] if carry is None else g[:, k] + carry
        carry = totals[:, k] if carry is None else carry + totals[:, k]
    return jnp.stack(out, axis=1).reshape(n, width)


def _gla_kernel(q_ref, k_ref, v_ref, gate_ref, cumf_ref, cumb_ref, g_ref, o_ref, stf_ref, stb_ref):
    c = GLA_CHUNK
    n_chunks = q_ref.shape[1] // c
    rows = lambda n: pl.ds(pl.multiple_of(n * c, c), c)
    cumf_ref, cumb_ref = cumf_ref.at[0], cumb_ref.at[0]
    gsz = GLA_GROUP * c
    row = lax.broadcasted_iota(jnp.int32, (gsz, gsz), 0)
    col = lax.broadcasted_iota(jnp.int32, (gsz, gsz), 1)
    same_chunk = (row // c) == (col // c)
    fwd = (cumf_ref, stf_ref, c - 1, same_chunk & (row >= col))
    bwd = (cumb_ref, stb_ref, 0, same_chunk & (row <= col))

    def step(n, state, direction):
        cum_ref, st_ref, last_row, _ = direction
        st_ref[n] = state.astype(BF16)
        cum = cum_ref[rows(n), :]
        last = cum[last_row:last_row + 1, :]
        kd = (k_ref[0, rows(n), :] * jnp.exp(last - cum)).astype(BF16)
        kv = lax.dot_general(v_ref[0, rows(n), :], kd, TN_DIMS, preferred_element_type=F32)
        return jnp.exp(last) * state + kv

    def scan(i, states):
        return step(i, states[0], fwd), step(n_chunks - 1 - i, states[1], bwd)

    zero = jnp.zeros((GLA_V, GLA_QK), F32)
    lax.fori_loop(0, n_chunks, scan, (zero, zero), unroll=32)

    def attend(g, direction):
        cum_ref, st_ref, _, keep = direction
        grows = pl.ds(pl.multiple_of(g * gsz, gsz), gsz)
        cum = cum_ref[grows, :]
        qg = (q_ref[0, grows, :] * jnp.exp(cum)).astype(BF16)
        kk = (k_ref[0, grows, :] * jnp.exp(-cum)).astype(BF16)
        sc = lax.dot_general(qg, kk, NT_DIMS, preferred_element_type=F32)
        intra = _dot(jnp.where(keep, sc, 0.0).astype(BF16), v_ref[0, grows, :])
        inter = [lax.dot_general(qg[i * c:(i + 1) * c], st_ref[g * GLA_GROUP + i], NT_DIMS,
                                 preferred_element_type=F32) for i in range(GLA_GROUP)]
        return intra + jnp.concatenate(inter, axis=0)

    def out(g, carry):
        grows = pl.ds(pl.multiple_of(g * gsz, gsz), gsz)
        o = attend(g, fwd) + attend(g, bwd)
        o_ref[0, grows, :] = (_rms(o, g_ref[...]) * _silu(gate_ref[0, grows, :])).astype(BF16)
        return carry

    lax.fori_loop(0, n_chunks // GLA_GROUP, out, 0, unroll=16)


def _gla(q, k, v, gate, cum_f, cum_b, g):
    b, s, _ = q.shape
    qk = pl.BlockSpec((1, s, GLA_QK), lambda bi, h: (bi, 0, h))
    vg = pl.BlockSpec((1, s, GLA_V), lambda bi, h: (bi, 0, h))
    return pl.pallas_call(
        _gla_kernel,
        grid=(b, GLA_HEADS),
        in_specs=[qk, qk, vg, vg, qk, qk, _resident((1, GLA_V))],
        out_specs=vg,
        out_shape=jax.ShapeDtypeStruct((b, s, GLA_HEADS * GLA_V), BF16),
        scratch_shapes=[pltpu.VMEM((s // GLA_CHUNK, GLA_V, GLA_QK), BF16),
                        pltpu.VMEM((s // GLA_CHUNK, GLA_V, GLA_QK), BF16)],
        compiler_params=_params(2),
        name="gla",
    )(q, k, v, gate, cum_f, cum_b, g.reshape(1, -1))


def kernel(x, positions, ffn1_norm, ffn1_w_gate, ffn1_w_up, ffn1_w_down, ffn2_norm, ffn2_w_gate, ffn2_w_up, ffn2_w_down, ab_norm, ab_w_in, da_lq1, da_lk1, da_lq2, da_lk2, da_norm, ret_logit_f, ret_logit_b, ret_norm, ab_w_out, c_norm, c_w_in, gla_w2_f, gla_b_f, gla_w2_b, gla_b_b, gla_norm, c_w_out, final_norm):
    b, s, d = x.shape
    t = b * s
    tables = _rope_tables(positions)
    x = x.reshape(t, d)
    seq3 = lambda a: a.reshape(b, s, -1)
    tok2 = lambda a: a.reshape(t, -1)
    ffn_casts = lambda wg, wu, wd, layer: [(wg, layer, None), (wu, layer, None), (wd, layer, None)]
    ffn1 = (ffn1_w_gate[0].astype(BF16), ffn1_w_up[0].astype(BF16), ffn1_w_down[0].astype(BF16))
    c_main = 2 * GLA_HEADS * (GLA_QK + GLA_V)
    for layer in range(DEPTH):
        i = layer // 2
        casts = ffn_casts(ffn2_w_gate, ffn2_w_up, ffn2_w_down, layer)
        if layer % 2 == 0:
            casts += [(ab_w_in, i, None), (ab_w_out, i, None)]
        else:
            casts += [(c_w_in, i, None), (c_w_out, i, None), (c_w_in, i, c_main)]
        x, *cast = _ffn(x, ffn1_norm[layer], *ffn1, final_norm, final_norm=False, casts=casts)
        ffn2, (w_in, w_out, *w_lr) = cast[:3], cast[3:]
        if layer % 2 == 0:
            lam_init = 0.8 - 0.6 * math.exp(-0.3 * layer)
            qa, ka, va, qr, kr, vr, gr = _proj_ab(x, ab_norm[i], w_in, tables, s)
            oa = _diff_attn(seq3(qa), seq3(ka), seq3(va), da_lq1[i], da_lk1[i], da_lq2[i], da_lk2[i],
                            da_norm[i], lam_init)
            orr = _retention(seq3(qr), seq3(kr), seq3(vr), seq3(gr), ret_logit_f[i], ret_logit_b[i],
                             ret_norm[i])
            mixed = [tok2(oa), tok2(orr)]
        else:
            q, k, v, gate, cum_f, cum_b = _proj_c(x, c_norm[i], w_in, *w_lr, gla_w2_f[i], gla_b_f[i],
                                                  gla_w2_b[i], gla_b_b[i])
            o = _gla(seq3(q), seq3(k), seq3(v), seq3(gate), seq3(cum_f), seq3(cum_b), gla_norm[i])
            mixed = [tok2(o)]
        casts = ffn_casts(ffn1_w_gate, ffn1_w_up, ffn1_w_down, layer + 1) if layer + 1 < DEPTH else []
        x, *cast = _ffn(x, ffn2_norm[layer], *ffn2, final_norm, final_norm=(layer == DEPTH - 1),
                        mixed=mixed, w_out=w_out, casts=casts)
        if cast:
            ffn1 = cast
    return x.reshape(b, s, d)
```

```python
import functools
import math

import jax
import jax.numpy as jnp
import numpy as np
from jax import lax
from jax.experimental import pallas as pl
from jax.experimental.pallas import tpu as pltpu

D_FF = 2816
DEPTH = 2
DA_HEADS = 4
DA_DIM = 64
DA_VDIM = 2 * DA_DIM
RET_HEADS = 4
RET_QK = 64
RET_V = 128
RET_CHUNK = 128
RET_THETA = 10000.0
GLA_HEADS = 4
GLA_QK = 128
GLA_V = 256
GLA_RANK = 16
GLA_TAU = 16.0
GLA_CHUNK = 64
ROPE_THETA = 500000.0
ROPE_FRAC = 4
EPS = 1e-6

LANES = 128
SUBLANES = 8
BF16_ROWS = 2 * SUBLANES
V7X_VMEM_LIMIT_BYTES = 56 * 1024 * 1024

ROPE_ROW_TILE = 512
FFN_TOKEN_TILE = 512
PROJ_AB_TOKEN_TILE = 1024
PROJ_C_TOKEN_TILE = 512
DA_QUERY_TILE = 1024

F32 = jnp.float32
BF16 = jnp.bfloat16
NT_DIMS = (((1,), (1,)), ((), ()))
TN_DIMS = (((0,), (0,)), ((), ()))


def _params(n_grid_dims):
    return pltpu.CompilerParams(
        dimension_semantics=("arbitrary",) * n_grid_dims,
        vmem_limit_bytes=V7X_VMEM_LIMIT_BYTES,
    )


def _resident(shape, index_map=None):
    if index_map is None:
        index_map = lambda *_: (0,) * len(shape)
    return pl.BlockSpec(shape, index_map, pipeline_mode=pl.Buffered(1))


def _rms(x, g):
    return x * lax.rsqrt(jnp.mean(x * x, axis=-1, keepdims=True) + EPS) * g


def _silu(x):
    return x * jax.nn.sigmoid(x)


def _log_sigmoid(x):
    return jnp.minimum(x, 0.0) - jnp.log(1.0 + jnp.exp(-jnp.abs(x)))


def _dot(a, b):
    return jnp.dot(a, b, preferred_element_type=F32)


def _cast_operands(casts, steps):
    in_specs, args, out_specs, out_shape = [], [], [], []
    for w, layer, tail in casts:
        _, rows, cols = w.shape
        cols_in, cols_out = (cols, LANES) if tail is not None else (cols // LANES * LANES,) * 2
        n_slabs = max(n for n in range(1, steps + 1) if rows % (n * BF16_ROWS) == 0)
        slab_of = functools.partial(lambda n, i: jnp.minimum(i, n - 1), n_slabs)
        in_specs.append(pl.BlockSpec((None, rows // n_slabs, cols_in),
                                     functools.partial(lambda l, f, i: (l, f(i), 0), layer, slab_of)))
        args.append(w)
        out_specs.append(pl.BlockSpec((rows // n_slabs, cols_out),
                                      functools.partial(lambda f, i: (f(i), 0), slab_of)))
        out_shape.append(jax.ShapeDtypeStruct((rows, cols_out), BF16))
    return in_specs, args, out_specs, out_shape, tuple(c[2] for c in casts)


def _run_casts(src_refs, dst_refs, tails):
    for src_ref, dst_ref, tail in zip(src_refs, dst_refs, tails):
        w = src_ref[...].astype(BF16)
        if tail is not None:
            w = jnp.concatenate([w[:, tail:], jnp.zeros((w.shape[0], LANES - (w.shape[1] - tail)), BF16)],
                                axis=1)
        dst_ref[...] = w


def _rope_table_kernel(*refs, cast_tails):
    n_casts = len(cast_tails)
    pos_ref, inv_a_ref, inv_r_ref = refs[:3]
    ca_ref, sa_up_ref, sa_dn_ref, cr_ref, sr_up_ref, sr_dn_ref = refs[3 + n_casts:9 + n_casts]
    _run_casts(refs[3:3 + n_casts], refs[9 + n_casts:], cast_tails)
    pos = pos_ref[...]
    lane = lax.broadcasted_iota(jnp.int32, (pos.shape[0], LANES), 1)
    rot = DA_DIM // ROPE_FRAC
    j = lane % DA_DIM
    ang = pos * inv_a_ref[...]
    c, s = jnp.cos(ang), jnp.sin(ang)
    ca_ref[...] = jnp.where(j < rot, c, 1.0)
    sa_up_ref[...] = jnp.where(j < rot // 2, -s, 0.0)
    sa_dn_ref[...] = jnp.where((j >= rot // 2) & (j < rot), s, 0.0)
    j = lane % RET_QK
    ang = pos * inv_r_ref[...]
    c, s = jnp.cos(ang), jnp.sin(ang)
    cr_ref[...] = c
    sr_up_ref[...] = jnp.where(j < RET_QK // 2, -s, 0.0)
    sr_dn_ref[...] = jnp.where(j >= RET_QK // 2, s, 0.0)


def _rope_tables(positions, casts):
    seq = positions.shape[0]
    ts = ROPE_ROW_TILE
    cast_in, cast_args, cast_out, cast_shape, tails = _cast_operands(casts, seq // ts)
    rot = DA_DIM // ROPE_FRAC
    inv_a = 1.0 / (ROPE_THETA ** (jnp.arange(0, rot, 2, dtype=F32) / rot))
    inv_r = 1.0 / (RET_THETA ** (jnp.arange(0, RET_QK, 2, dtype=F32) / RET_QK))
    lane = np.arange(LANES)
    inv_a_lanes = inv_a[(lane % DA_DIM) % (rot // 2)][None, :]
    inv_r_lanes = inv_r[(lane % RET_QK) % (RET_QK // 2)][None, :]
    pos = positions.astype(F32).reshape(seq, 1)
    row = pl.BlockSpec((ts, LANES), lambda i: (i, 0))
    vec = pl.BlockSpec((1, LANES), lambda i: (0, 0))
    return pl.pallas_call(
        functools.partial(_rope_table_kernel, cast_tails=tails),
        grid=(seq // ts,),
        in_specs=[pl.BlockSpec((ts, 1), lambda i: (i, 0)), vec, vec] + cast_in,
        out_specs=[row] * 6 + cast_out,
        out_shape=[jax.ShapeDtypeStruct((seq, LANES), F32)] * 6 + cast_shape,
        compiler_params=_params(1),
        name="rope_tables",
    )(pos, inv_a_lanes, inv_r_lanes, *cast_args)


def _rope(x, c, s_up, s_dn, half):
    out = []
    for b in range(x.shape[1] // LANES):
        xb = x[:, b * LANES:(b + 1) * LANES]
        up = pltpu.roll(xb, LANES - half, 1)
        dn = pltpu.roll(xb, half, 1)
        out.append(xb * c + up * s_up + dn * s_dn)
    return jnp.concatenate(out, axis=1)


def _ffn_kernel(*refs, n_mixed, cast_tails, final_norm):
    n_casts = len(cast_tails)
    n_in = 1 + 2 * n_mixed + 5 + n_casts
    x_ref, mixed = refs[0], refs[1:1 + 2 * n_mixed]
    g_ref, wg_ref, wu_ref, wd_ref, fg_ref = refs[1 + 2 * n_mixed:n_in - n_casts]
    o_ref = refs[n_in]
    _run_casts(refs[n_in - n_casts:n_in], refs[n_in + 1:], cast_tails)
    x = x_ref[...]
    for a_ref, w_ref in zip(mixed[::2], mixed[1::2]):
        x = x + _dot(a_ref[...], w_ref[...])
    xn = _rms(x, g_ref[...]).astype(BF16)
    gate = _dot(xn, wg_ref[...])
    up = _dot(xn, wu_ref[...])
    h = (_silu(gate) * up).astype(BF16)
    out = x + 0.5 * _dot(h, wd_ref[...])
    if final_norm:
        out = _rms(out, fg_ref[...])
    o_ref[...] = out


def _ffn(x, g, wg, wu, wd, fg, *, final_norm, mixed=(), w_out=None, casts=()):
    t, d = x.shape
    tm = FFN_TOKEN_TILE
    steps = t // tm
    tok = lambda wd_: pl.BlockSpec((tm, wd_), lambda i: (i, 0))
    in_specs, args, row = [tok(d)], [x], 0
    for a in mixed:
        wa = a.shape[1]
        in_specs += [tok(wa), _resident((wa, d), functools.partial(lambda r, i: (r, 0), row // wa))]
        args += [a, w_out]
        row += wa
    in_specs += [_resident((1, d)), _resident((d, D_FF)), _resident((d, D_FF)), _resident((D_FF, d)),
                 _resident((1, d))]
    args += [g.reshape(1, d), wg, wu, wd, fg.reshape(1, d)]
    cast_in, cast_args, cast_out, cast_shape, tails = _cast_operands(casts, steps)
    return pl.pallas_call(
        functools.partial(_ffn_kernel, n_mixed=len(mixed), cast_tails=tails, final_norm=final_norm),
        grid=(steps,),
        in_specs=in_specs + cast_in,
        out_specs=[tok(d)] + cast_out,
        out_shape=[jax.ShapeDtypeStruct((t, d), F32)] + cast_shape,
        compiler_params=_params(1),
        name="ffn",
    )(*args, *cast_args)


def _proj_ab_kernel(x_ref, g_ref, w_ref, ca_ref, sa_up_ref, sa_dn_ref, cr_ref, sr_up_ref, sr_dn_ref,
                    qa_ref, ka_ref, va_ref, qr_ref, kr_ref, vr_ref, gr_ref):
    xn = _rms(x_ref[...], g_ref[...]).astype(BF16)
    y = _dot(xn, w_ref[...])
    qa_w = DA_HEADS * 2 * DA_DIM
    va_w = DA_HEADS * DA_VDIM
    qr_w = RET_HEADS * RET_QK
    vr_w = RET_HEADS * RET_V
    o = 0
    qa, o = y[:, o:o + qa_w], o + qa_w
    ka, o = y[:, o:o + qa_w], o + qa_w
    va, o = y[:, o:o + va_w], o + va_w
    qr, o = y[:, o:o + qr_w], o + qr_w
    kr, o = y[:, o:o + qr_w], o + qr_w
    vr, o = y[:, o:o + vr_w], o + vr_w
    gr = y[:, o:o + vr_w]
    rope_a = functools.partial(_rope, c=ca_ref[...], s_up=sa_up_ref[...], s_dn=sa_dn_ref[...],
                               half=DA_DIM // ROPE_FRAC // 2)
    rope_r = functools.partial(_rope, c=cr_ref[...], s_up=sr_up_ref[...], s_dn=sr_dn_ref[...],
                               half=RET_QK // 2)
    qa_ref[...] = (rope_a(qa) * (DA_DIM ** -0.5 * math.log2(math.e))).astype(BF16)
    ka_ref[...] = rope_a(ka).astype(BF16)
    va_ref[...] = va.astype(BF16)
    qr_ref[...] = rope_r(qr)
    kr_ref[...] = rope_r(kr) * RET_QK ** -0.5
    vr_ref[...] = vr.astype(BF16)
    gr_ref[...] = gr


def _proj_ab(x, g, w, tables, seq):
    t, d = x.shape
    tm = PROJ_AB_TOKEN_TILE
    n_in = w.shape[1]
    widths = (DA_HEADS * 2 * DA_DIM, DA_HEADS * 2 * DA_DIM, DA_HEADS * DA_VDIM,
              RET_HEADS * RET_QK, RET_HEADS * RET_QK, RET_HEADS * RET_V, RET_HEADS * RET_V)
    dtypes = (BF16, BF16, BF16, F32, F32, BF16, F32)
    tok = lambda wd: pl.BlockSpec((tm, wd), lambda i: (i, 0))
    tab = pl.BlockSpec((tm, LANES), lambda i: (i % (seq // tm), 0))
    return pl.pallas_call(
        _proj_ab_kernel,
        grid=(t // tm,),
        in_specs=[tok(d), _resident((1, d)), _resident((d, n_in))] + [tab] * 6,
        out_specs=[tok(wd) for wd in widths],
        out_shape=[jax.ShapeDtypeStruct((t, wd), dt) for wd, dt in zip(widths, dtypes)],
        compiler_params=_params(1),
        name="proj_ab",
    )(x, g.reshape(1, d), w, *tables)


PROJ_SLAB = 256


def _proj_c_kernel(x_ref, g_ref, w_ref, wlr_ref, w2f_ref, w2b_ref, bf_ref, bb_ref,
                   q_ref, k_ref, v_ref, gate_ref, cumf_ref, cumb_ref):
    xn = _rms(x_ref[...], g_ref[...]).astype(BF16)
    lr = _dot(xn, wlr_ref[...]).astype(BF16)
    qk_w = GLA_HEADS * GLA_QK
    v_w = GLA_HEADS * GLA_V
    outs = ((q_ref, 0, lambda y: y * GLA_QK ** -0.5), (k_ref, qk_w, lambda y: y),
            (v_ref, 2 * qk_w, lambda y: y.astype(BF16)), (gate_ref, 2 * qk_w + v_w, lambda y: y))
    tm, n_main = x_ref.shape[0], w_ref.shape[1]

    def slab(lo):
        hi = lo + PROJ_SLAB
        y = _dot(xn, w_ref[:, lo:hi])
        for ref, first, post in outs:
            a, b = max(lo, first), min(hi, first + ref.shape[1])
            if a < b:
                ref[:, a - first:b - first] = post(y[:, a - lo:b - lo])

    def gates(h, direction, rows):
        w2_ref, b_ref, cum_ref, reverse = direction
        cols = slice(h * GLA_QK, (h + 1) * GLA_QK)
        log_a = _log_sigmoid(_dot(lr[rows], w2_ref[:, cols]) + b_ref[:, cols]) / GLA_TAU
        cum_ref[rows, cols] = _chunk_cumsum(log_a, reverse)

    directions = ((w2f_ref, bf_ref, cumf_ref, False), (w2b_ref, bb_ref, cumb_ref, True))
    row_blocks = [slice(r, r + tm // 2) for r in (0, tm // 2)]
    mxu_work = [functools.partial(slab, lo) for lo in range(0, n_main, PROJ_SLAB)]
    vpu_work = [functools.partial(gates, h, d, rows) for h in range(GLA_HEADS) for d in directions
                for rows in row_blocks]
    order = sorted([(i / len(mxu_work), 0, f) for i, f in enumerate(mxu_work)]
                   + [(i / len(vpu_work), 1, f) for i, f in enumerate(vpu_work)], key=lambda e: e[:2])
    for _, _, work in order:
        work()


def _proj_c(x, g, w_main, w_lr, w2_f, b_f, w2_b, b_b):
    t, d = x.shape
    tm = PROJ_C_TOKEN_TILE
    qk_w = GLA_HEADS * GLA_QK
    v_w = GLA_HEADS * GLA_V
    n_main = w_main.shape[1]
    w2f = jnp.zeros((LANES, qk_w), BF16).at[:GLA_RANK].set(w2_f.astype(BF16))
    w2b = jnp.zeros((LANES, qk_w), BF16).at[GLA_RANK:2 * GLA_RANK].set(w2_b.astype(BF16))
    widths = (qk_w, qk_w, v_w, v_w, qk_w, qk_w)
    dtypes = (F32, F32, BF16, F32, F32, F32)
    tok = lambda wd: pl.BlockSpec((tm, wd), lambda i: (i, 0))
    return pl.pallas_call(
        _proj_c_kernel,
        grid=(t // tm,),
        in_specs=[tok(d), _resident((1, d)), _resident((d, n_main)), _resident((d, LANES)),
                  _resident((LANES, qk_w)), _resident((LANES, qk_w)), _resident((1, qk_w)),
                  _resident((1, qk_w))],
        out_specs=[tok(wd) for wd in widths],
        out_shape=[jax.ShapeDtypeStruct((t, wd), dt) for wd, dt in zip(widths, dtypes)],
        compiler_params=_params(1),
        name="proj_c",
    )(x, g.reshape(1, d), w_main, w_lr, w2f, w2b, b_f.reshape(1, -1), b_b.reshape(1, -1))


DA_KEY_TILE = 256


def _diff_attn_kernel(q_ref, k_ref, v_ref, lq1_ref, lk1_ref, lq2_ref, lk2_ref, g_ref, o_ref, *, lam_init):
    q = q_ref[0]
    lane = lax.broadcasted_iota(jnp.int32, q.shape, 1)
    zero = jnp.zeros_like(q)
    q_comp = (jnp.where(lane < DA_DIM, q, zero), jnp.where(lane >= DA_DIM, q, zero))
    scores = [lax.dot_general(qc, k_ref[0], NT_DIMS, preferred_element_type=F32) for qc in q_comp]
    row_max = [jnp.max(s, axis=-1, keepdims=True) for s in scores]
    tq, seq = scores[0].shape
    ones = jnp.ones((DA_KEY_TILE, LANES), BF16)
    pv = [jnp.zeros((tq, DA_VDIM + LANES), F32) for _ in scores]
    for j in range(seq // DA_KEY_TILE):
        cols = slice(j * DA_KEY_TILE, (j + 1) * DA_KEY_TILE)
        v_ones = jnp.concatenate([v_ref[0, cols, :], ones], axis=1)
        for c, s in enumerate(scores):
            pj = jnp.exp2(s[:, cols] - row_max[c]).astype(BF16)
            pv[c] = pv[c] + _dot(pj, v_ones)
    lam = (jnp.exp(jnp.sum(lq1_ref[...] * lk1_ref[...], axis=-1, keepdims=True))
           - jnp.exp(jnp.sum(lq2_ref[...] * lk2_ref[...], axis=-1, keepdims=True)) + lam_init)
    soft = [x[:, :DA_VDIM] / x[:, DA_VDIM:] for x in pv]
    o = soft[0] - lam * soft[1]
    o_ref[0] = (_rms(o, g_ref[...]) * (1.0 - lam_init)).astype(BF16)


def _diff_attn(q, k, v, lq1, lk1, lq2, lk2, g, lam_init):
    b, s, _ = q.shape
    tq = DA_QUERY_TILE
    hw = 2 * DA_DIM
    vec = lambda n: _resident((1, n))
    return pl.pallas_call(
        functools.partial(_diff_attn_kernel, lam_init=lam_init),
        grid=(b, DA_HEADS, s // tq),
        in_specs=[pl.BlockSpec((1, tq, hw), lambda bi, h, qi: (bi, qi, h)),
                  pl.BlockSpec((1, s, hw), lambda bi, h, qi: (bi, 0, h)),
                  pl.BlockSpec((1, s, DA_VDIM), lambda bi, h, qi: (bi, 0, h)),
                  vec(DA_DIM), vec(DA_DIM), vec(DA_DIM), vec(DA_DIM), vec(DA_VDIM)],
        out_specs=pl.BlockSpec((1, tq, DA_VDIM), lambda bi, h, qi: (bi, qi, h)),
        out_shape=jax.ShapeDtypeStruct((b, s, DA_HEADS * DA_VDIM), BF16),
        compiler_params=_params(3),
        name="diff_attn",
    )(q, k, v, lq1.reshape(1, -1), lk1.reshape(1, -1), lq2.reshape(1, -1), lk2.reshape(1, -1),
      g.reshape(1, -1))


RET_PAIR = LANES // RET_QK


def _retention_kernel(q_ref, k_ref, v_ref, gate_ref, lf_ref, lb_ref, g_ref, o_ref, st_ref):
    c = RET_CHUNK
    n_chunks = q_ref.shape[1] // c
    pw = RET_PAIR * RET_V
    row = lax.broadcasted_iota(jnp.int32, (c, c), 0).astype(F32)
    col = lax.broadcasted_iota(jnp.int32, (c, c), 1).astype(F32)
    lane_head = lax.broadcasted_iota(jnp.int32, (c, LANES), 1) // RET_QK
    rows = lambda n: pl.ds(pl.multiple_of(n * c, c), c)
    heads = range(RET_PAIR)
    lg_f = [_log_sigmoid(lf_ref[j]) for j in heads]
    lg_b = [_log_sigmoid(lb_ref[j]) for j in heads]

    def per_lane(lg):
        out = lg[0]
        for j in heads[1:]:
            out = jnp.where(lane_head[:1] == j, lg[j], out)
        return out

    per_col = lambda lg: jnp.concatenate([jnp.broadcast_to(x, (1, RET_V)) for x in lg], axis=1)
    diff = row - col
    decay = [jnp.where(diff >= 0, jnp.exp(lg_f[j] * jnp.maximum(diff, 0.0)), 0.0)
             + jnp.where(diff <= 0, jnp.exp(lg_b[j] * jnp.maximum(-diff, 0.0)), 0.0) for j in heads]
    lgl_f, lgl_b = per_lane(lg_f), per_lane(lg_b)
    q_dec_f = jnp.exp(lgl_f * (row + 1.0))
    k_dec_f = jnp.exp(lgl_f * (c - 1.0 - row))
    q_dec_b = jnp.exp(lgl_b * (c - row))
    k_dec_b = jnp.exp(lgl_b * row)
    chunk_dec_f = jnp.exp(per_col(lg_f) * c)
    chunk_dec_b = jnp.exp(per_col(lg_b) * c)
    own_block = (lax.broadcasted_iota(jnp.int32, (LANES, pw), 0) // RET_QK
                 == lax.broadcasted_iota(jnp.int32, (LANES, pw), 1) // RET_V)

    def kv(n, k_dec):
        kd = (k_ref[0, rows(n), :] * k_dec).astype(BF16)
        out = lax.dot_general(kd, v_ref[0, rows(n), :], TN_DIMS, preferred_element_type=F32)
        return jnp.where(own_block, out, 0.0)

    def scan(i, states):
        sf, sb = states
        nf, nb = i, n_chunks - 1 - i
        st_ref[nf, :LANES, :] = sf.astype(BF16)
        st_ref[nb, LANES:, :] = sb.astype(BF16)
        return chunk_dec_f * sf + kv(nf, k_dec_f), chunk_dec_b * sb + kv(nb, k_dec_b)

    zero = jnp.zeros((LANES, pw), F32)
    lax.fori_loop(0, n_chunks, scan, (zero, zero), unroll=32)

    def out(n, carry):
        qc = q_ref[0, rows(n), :]
        kc = k_ref[0, rows(n), :].astype(BF16)
        vc = v_ref[0, rows(n), :]
        q_heads = jnp.concatenate([jnp.where(lane_head == j, qc, 0.0) for j in heads], axis=0)
        sc = lax.dot_general(q_heads.astype(BF16), kc, NT_DIMS, preferred_element_type=F32)
        inter = _dot(jnp.concatenate([qc * q_dec_f, qc * q_dec_b], axis=1).astype(BF16), st_ref[n])
        o = []
        for j in heads:
            a = (sc[j * c:(j + 1) * c] * decay[j]).astype(BF16)
            oj = _dot(a, vc[:, j * RET_V:(j + 1) * RET_V]) + inter[:, j * RET_V:(j + 1) * RET_V]
            o.append(_rms(oj, g_ref[...]))
        o = jnp.concatenate(o, axis=1) * _silu(gate_ref[0, rows(n), :])
        o_ref[0, rows(n), :] = o.astype(BF16)
        return carry

    lax.fori_loop(0, n_chunks, out, 0, unroll=32)


def _retention(q, k, v, gate, logit_f, logit_b, g):
    b, s, _ = q.shape
    n_chunks = s // RET_CHUNK
    pw = RET_PAIR * RET_V
    lanes = lambda x: jnp.broadcast_to(x.astype(F32)[:, None, None], (RET_HEADS, 1, LANES))
    qk = pl.BlockSpec((1, s, LANES), lambda bi, p: (bi, 0, p))
    vg = pl.BlockSpec((1, s, pw), lambda bi, p: (bi, 0, p))
    logit = pl.BlockSpec((RET_PAIR, 1, LANES), lambda bi, p: (p, 0, 0))
    return pl.pallas_call(
        _retention_kernel,
        grid=(b, RET_HEADS // RET_PAIR),
        in_specs=[qk, qk, vg, vg, logit, logit, _resident((1, RET_V))],
        out_specs=vg,
        out_shape=jax.ShapeDtypeStruct((b, s, RET_HEADS * RET_V), BF16),
        scratch_shapes=[pltpu.VMEM((n_chunks, 2 * LANES, pw), BF16)],
        compiler_params=_params(2),
        name="retention",
    )(q, k, v, gate, lanes(logit_f), lanes(logit_b), g.reshape(1, -1))


GLA_GROUP = 4


def _chunk_cumsum(x, reverse):
    n, width = x.shape
    sub = SUBLANES
    per_chunk = GLA_CHUNK // sub
    g = x.reshape(n // sub, sub, width)
    pos = lax.broadcasted_iota(jnp.int32, g.shape, 1)
    step = 1
    while step < sub:
        if reverse:
            g = g + jnp.where(pos < sub - step, pltpu.roll(g, sub - step, 1), 0.0)
        else:
            g = g + jnp.where(pos >= step, pltpu.roll(g, step, 1), 0.0)
        step *= 2
    g = g.reshape(n // GLA_CHUNK, per_chunk, sub, width)
    edge = 0 if reverse else sub - 1
    totals = jnp.broadcast_to(g[:, :, edge:edge + 1, :], g.shape)
    order = range(per_chunk - 1, -1, -1) if reverse else range(per_chunk)
    out, carry = [None] * per_chunk, None
    for k in order:
        out[k] = g[:, k] if carry is None else g[:, k] + carry
        carry = totals[:, k] if carry is None else carry + totals[:, k]
    return jnp.stack(out, axis=1).reshape(n, width)


def _gla_kernel(q_ref, k_ref, v_ref, gate_ref, cumf_ref, cumb_ref, g_ref, o_ref, stf_ref, stb_ref):
    c = GLA_CHUNK
    n_chunks = q_ref.shape[1] // c
    rows = lambda n: pl.ds(pl.multiple_of(n * c, c), c)
    cumf_ref, cumb_ref = cumf_ref.at[0], cumb_ref.at[0]
    gsz = GLA_GROUP * c
    row = lax.broadcasted_iota(jnp.int32, (gsz, gsz), 0)
    col = lax.broadcasted_iota(jnp.int32, (gsz, gsz), 1)
    same_chunk = (row // c) == (col // c)
    fwd = (cumf_ref, stf_ref, c - 1, same_chunk & (row >= col))
    bwd = (cumb_ref, stb_ref, 0, same_chunk & (row <= col))

    def step(n, state, direction):
        cum_ref, st_ref, last_row, _ = direction
        st_ref[n] = state.astype(BF16)
        cum = cum_ref[rows(n), :]
        last = cum[last_row:last_row + 1, :]
        kd = (k_ref[0, rows(n), :] * jnp.exp(last - cum)).astype(BF16)
        kv = lax.dot_general(v_ref[0, rows(n), :], kd, TN_DIMS, preferred_element_type=F32)
        return jnp.exp(last) * state + kv

    def scan(i, states):
        return step(i, states[0], fwd), step(n_chunks - 1 - i, states[1], bwd)

    zero = jnp.zeros((GLA_V, GLA_QK), F32)
    lax.fori_loop(0, n_chunks, scan, (zero, zero), unroll=32)

    def attend(g, direction):
        cum_ref, st_ref, _, keep = direction
        grows = pl.ds(pl.multiple_of(g * gsz, gsz), gsz)
        cum = cum_ref[grows, :]
        qg = (q_ref[0, grows, :] * jnp.exp(cum)).astype(BF16)
        kk = (k_ref[0, grows, :] * jnp.exp(-cum)).astype(BF16)
        sc = lax.dot_general(qg, kk, NT_DIMS, preferred_element_type=F32)
        intra = _dot(jnp.where(keep, sc, 0.0).astype(BF16), v_ref[0, grows, :])
        inter = [lax.dot_general(qg[i * c:(i + 1) * c], st_ref[g * GLA_GROUP + i], NT_DIMS,
                                 preferred_element_type=F32) for i in range(GLA_GROUP)]
        return intra + jnp.concatenate(inter, axis=0)

    def out(g, carry):
        grows = pl.ds(pl.multiple_of(g * gsz, gsz), gsz)
        o = attend(g, fwd) + attend(g, bwd)
        o_ref[0, grows, :] = (_rms(o, g_ref[...]) * _silu(gate_ref[0, grows, :])).astype(BF16)
        return carry

    lax.fori_loop(0, n_chunks // GLA_GROUP, out, 0, unroll=16)


def _gla(q, k, v, gate, cum_f, cum_b, g):
    b, s, _ = q.shape
    qk = pl.BlockSpec((1, s, GLA_QK), lambda bi, h: (bi, 0, h))
    vg = pl.BlockSpec((1, s, GLA_V), lambda bi, h: (bi, 0, h))
    return pl.pallas_call(
        _gla_kernel,
        grid=(b, GLA_HEADS),
        in_specs=[qk, qk, vg, vg, qk, qk, _resident((1, GLA_V))],
        out_specs=vg,
        out_shape=jax.ShapeDtypeStruct((b, s, GLA_HEADS * GLA_V), BF16),
        scratch_shapes=[pltpu.VMEM((s // GLA_CHUNK, GLA_V, GLA_QK), BF16),
                        pltpu.VMEM((s // GLA_CHUNK, GLA_V, GLA_QK), BF16)],
        compiler_params=_params(2),
        name="gla",
    )(q, k, v, gate, cum_f, cum_b, g.reshape(1, -1))


def kernel(x, positions, ffn1_norm, ffn1_w_gate, ffn1_w_up, ffn1_w_down, ffn2_norm, ffn2_w_gate, ffn2_w_up, ffn2_w_down, ab_norm, ab_w_in, da_lq1, da_lk1, da_lq2, da_lk2, da_norm, ret_logit_f, ret_logit_b, ret_norm, ab_w_out, c_norm, c_w_in, gla_w2_f, gla_b_f, gla_w2_b, gla_b_b, gla_norm, c_w_out, final_norm):
    b, s, d = x.shape
    t = b * s
    x = x.reshape(t, d)
    seq3 = lambda a: a.reshape(b, s, -1)
    tok2 = lambda a: a.reshape(t, -1)
    ffn_casts = lambda wg, wu, wd, layer: [(wg, layer, None), (wu, layer, None), (wd, layer, None)]
    *tables, wg, wu, wd = _rope_tables(positions, ffn_casts(ffn1_w_gate, ffn1_w_up, ffn1_w_down, 0))
    ffn1 = (wg, wu, wd)
    c_main = 2 * GLA_HEADS * (GLA_QK + GLA_V)
    for layer in range(DEPTH):
        i = layer // 2
        casts = ffn_casts(ffn2_w_gate, ffn2_w_up, ffn2_w_down, layer)
        if layer % 2 == 0:
            casts += [(ab_w_in, i, None), (ab_w_out, i, None)]
        else:
            casts += [(c_w_in, i, None), (c_w_out, i, None), (c_w_in, i, c_main)]
        x, *cast = _ffn(x, ffn1_norm[layer], *ffn1, final_norm, final_norm=False, casts=casts)
        ffn2, (w_in, w_out, *w_lr) = cast[:3], cast[3:]
        if layer % 2 == 0:
            lam_init = 0.8 - 0.6 * math.exp(-0.3 * layer)
            qa, ka, va, qr, kr, vr, gr = _proj_ab(x, ab_norm[i], w_in, tables, s)
            oa = _diff_attn(seq3(qa), seq3(ka), seq3(va), da_lq1[i], da_lk1[i], da_lq2[i], da_lk2[i],
                            da_norm[i], lam_init)
            orr = _retention(seq3(qr), seq3(kr), seq3(vr), seq3(gr), ret_logit_f[i], ret_logit_b[i],
                             ret_norm[i])
            mixed = [tok2(oa), tok2(orr)]
        else:
            q, k, v, gate, cum_f, cum_b = _proj_c(x, c_norm[i], w_in, *w_lr, gla_w2_f[i], gla_b_f[i],
                                                  gla_w2_b[i], gla_b_b[i])
            o = _gla(seq3(q), seq3(k), seq3(v), seq3(gate), seq3(cum_f), seq3(cum_b), gla_norm[i])
            mixed = [tok2(o)]
        casts = ffn_casts(ffn1_w_gate, ffn1_w_up, ffn1_w_down, layer + 1) if layer + 1 < DEPTH else []
        x, *cast = _ffn(x, ffn2_norm[layer], *ffn2, final_norm, final_norm=(layer == DEPTH - 1),
                        mixed=mixed, w_out=w_out, casts=casts)
        if cast:
            ffn1 = cast
    return x.reshape(b, s, d)
```
